```python
import jax, jax.numpy as jnp
from jax import lax
import numpy as np

D_MODEL = 2048
BATCH = 1
SEQ = 8192
DEPTH = 1

MEM_LEN = 256
RMS_EPS = 1e-5

SB_HEADS = 8
SB_HEAD_DIM = 128
SB_WIDTH = SB_HEADS * SB_HEAD_DIM
Q_BLOCK = 128

POOL_WINDOWS = (2, 4, 8, 16)
POOL_GROUPS = len(POOL_WINDOWS)
POOL_GROUP_DIM = 256
POOL_WIDTH = POOL_GROUPS * POOL_GROUP_DIM

N_BRANCH = 2
IN_WIDTH = 3 * SB_WIDTH + POOL_WIDTH

XA_HEADS = 4
XA_HEAD_DIM = 128
XA_WIDTH = XA_HEADS * XA_HEAD_DIM

N_EXPERTS = 32
TOP_K = 4
D_FF = D_MODEL
SWIGLU_ALPHA = 1.702
SWIGLU_LIMIT = 7.0
EXPERT_BLOCK = 256

kernel_name = "hybrid_stickbreak_pool_moe_block"


def rms_norm(x, g):
    x32 = x.astype(jnp.float32)
    y = x32 * lax.rsqrt(jnp.mean(x32 * x32, axis=-1, keepdims=True) + RMS_EPS)
    return (y * g.astype(jnp.float32)).astype(x.dtype)


def stick_breaking_attention(q, k, v):
    B, H, S, dh = q.shape
    nb = S // Q_BLOCK
    scale = dh ** -0.5
    k_pos = jnp.arange(S)
    q_blocks = q.reshape(B, H, nb, Q_BLOCK, dh).transpose(2, 0, 1, 3, 4)
    starts = jnp.arange(nb) * Q_BLOCK

    def one_block(args):
        q_blk, start = args
        z = jnp.einsum('bhqd,bhkd->bhqk', q_blk, k, preferred_element_type=jnp.float32) * scale
        q_pos = start + jnp.arange(Q_BLOCK)
        causal = k_pos[None, :] < q_pos[:, None]
        neg_log_keep = jnp.where(causal, jax.nn.softplus(z), 0.0)
        later = lax.cumsum(neg_log_keep, axis=3, reverse=True) - neg_log_keep
        log_w = jax.nn.log_sigmoid(z) - later
        w = jnp.where(causal, jnp.exp(log_w), 0.0)
        return jnp.einsum('bhqk,bhkd->bhqd', w.astype(v.dtype), v)

    out = lax.map(one_block, (q_blocks, starts))
    return out.transpose(1, 2, 0, 3, 4).reshape(B, H, S, dh)


def multiscale_pool(p, pool_lin, pool_scale):
    B, S, _ = p.shape
    pg = p.reshape(B, S, POOL_GROUPS, POOL_GROUP_DIM).astype(jnp.float32)
    csum = jnp.cumsum(pg, axis=1)
    pos = jnp.arange(S)
    outs = []
    for g, w in enumerate(POOL_WINDOWS):
        c = csum[:, :, g]
        c_prev = jnp.pad(c, ((0, 0), (w, 0), (0, 0)))[:, :S]
        count = jnp.minimum(pos + 1, w).astype(jnp.float32)[None, :, None]
        outs.append((c - c_prev) / count - pg[:, :, g])
    pooled = jnp.stack(outs, axis=2).astype(p.dtype)
    mixed = jnp.einsum('bsgc,gcd->bsgd', pooled, pool_lin)
    return mixed.reshape(B, S, POOL_WIDTH) * pool_scale


def memory_cross_attention(u, m, w_q, w_kv, w_o):
    B, S, _ = u.shape
    M = m.shape[1]
    q = (u @ w_q).reshape(B, S, XA_HEADS, XA_HEAD_DIM)
    kv = (m @ w_kv).reshape(B, M, 2, XA_HEADS, XA_HEAD_DIM)
    k, v = kv[:, :, 0], kv[:, :, 1]
    s = jnp.einsum('bshd,bmhd->bhsm', q, k, preferred_element_type=jnp.float32) * (XA_HEAD_DIM ** -0.5)
    p = jax.nn.softmax(s, axis=-1).astype(v.dtype)
    o = jnp.einsum('bhsm,bmhd->bshd', p, v).reshape(B, S, XA_WIDTH)
    return o @ w_o


def moe_ffn(t, w_router, b_router, w_up, b_up, w_down, b_down):
    B, S, D = t.shape
    N = B * S
    tok = t.reshape(N, D)
    logits = (tok @ w_router + b_router).astype(jnp.float32)
    top_logit, top_idx = lax.top_k(logits, TOP_K)
    gate = jax.nn.softmax(top_logit, axis=-1).astype(t.dtype)

    NK = N * TOP_K
    flat_e = top_idx.reshape(NK)
    flat_tok = jnp.arange(NK, dtype=jnp.int32) // TOP_K
    flat_g = gate.reshape(NK)
    order = jnp.argsort(flat_e)
    se, st, sg = flat_e[order], flat_tok[order], flat_g[order]

    counts = jnp.bincount(flat_e, length=N_EXPERTS)
    starts = jnp.cumsum(counts) - counts
    padded = (counts + EXPERT_BLOCK - 1) // EXPERT_BLOCK * EXPERT_BLOCK
    padded_ends = jnp.cumsum(padded)
    padded_starts = padded_ends - padded
    dest = padded_starts[se] + (jnp.arange(NK) - starts[se])

    nblk = -(-NK // EXPERT_BLOCK) + N_EXPERTS
    P = nblk * EXPERT_BLOCK
    row_tok = jnp.full((P,), N, jnp.int32).at[dest].set(st)
    row_gate = jnp.zeros((P,), t.dtype).at[dest].set(sg)
    blk_start = jnp.arange(nblk) * EXPERT_BLOCK
    blk_expert = jnp.minimum(jnp.sum(padded_ends[None, :] <= blk_start[:, None], axis=1), N_EXPERTS - 1)

    tok_pad = jnp.concatenate([tok, jnp.zeros((1, D), tok.dtype)], axis=0)
    rows = tok_pad[row_tok].reshape(nblk, EXPERT_BLOCK, D)

    def expert_block(args):
        xb, e = args
        hdn = xb @ w_up[e] + b_up[e]
        g_h = jnp.minimum(hdn[:, :D_FF], SWIGLU_LIMIT)
        l_h = jnp.clip(hdn[:, D_FF:], -SWIGLU_LIMIT, SWIGLU_LIMIT)
        act = g_h * jax.nn.sigmoid(SWIGLU_ALPHA * g_h) * (l_h + 1.0)
        return act @ w_down[e] + b_down[e]

    y_rows = lax.map(expert_block, (rows, blk_expert)).reshape(P, D)
    y = jax.ops.segment_sum(y_rows * row_gate[:, None], row_tok, num_segments=N + 1)[:N]
    return y.reshape(B, S, D)


def setup_inputs(seed: int = 0) -> dict:
    key = jax.random.key(seed)
    ks = jax.random.split(key, 24)
    f32 = jnp.float32
    L, D = DEPTH, D_MODEL

    def w(k, shape, fan_in):
        return jax.random.normal(k, shape, f32) * (fan_in ** -0.5)

    def gain(k, shape):
        return 1.0 + 0.02 * jax.random.normal(k, shape, f32)

    def small(k, shape):
        return 0.01 * jax.random.normal(k, shape, f32)

    return {
        "x": jax.random.normal(ks[0], (BATCH, SEQ, D), f32),
        "mem": jax.random.normal(ks[1], (BATCH, MEM_LEN, D), f32),
        "norm_mix": gain(ks[2], (L, D)),
        "w_in": w(ks[3], (L, D, IN_WIDTH), D),
        "pool_lin": w(ks[4], (L, POOL_GROUPS, POOL_GROUP_DIM, POOL_GROUP_DIM), POOL_GROUP_DIM),
        "pool_scale": gain(ks[5], (L, POOL_WIDTH)),
        "w_branch_a": w(ks[6], (L, SB_WIDTH, D), SB_WIDTH),
        "w_branch_b": w(ks[7], (L, POOL_WIDTH, D), POOL_WIDTH),
        "w_gate": w(ks[8], (L, D, N_BRANCH * D), D),
        "b_gate": small(ks[9], (L, N_BRANCH * D)),
        "w_mix_out": w(ks[10], (L, D, D), D),
        "xa_norm": gain(ks[11], (L, D)),
        "mem_norm": gain(ks[12], (L, D)),
        "xa_wq": w(ks[13], (L, D, XA_WIDTH), D),
        "xa_wkv": w(ks[14], (L, D, 2 * XA_WIDTH), D),
        "xa_wo": w(ks[15], (L, XA_WIDTH, D), XA_WIDTH),
        "moe_norm": gain(ks[16], (L, D)),
        "w_router": w(ks[17], (L, D, N_EXPERTS), D),
        "b_router": small(ks[18], (L, N_EXPERTS)),
        "w_up": w(ks[19], (L, N_EXPERTS, D, 2 * D_FF), D),
        "b_up": small(ks[20], (L, N_EXPERTS, 2 * D_FF)),
        "w_down": w(ks[21], (L, N_EXPERTS, D_FF, D), D_FF),
        "b_down": small(ks[22], (L, N_EXPERTS, D)),
        "final_norm": gain(ks[23], (D,)),
    }


def reference(x, mem, norm_mix, w_in, pool_lin, pool_scale, w_branch_a, w_branch_b, w_gate, b_gate,
              w_mix_out, xa_norm, mem_norm, xa_wq, xa_wkv, xa_wo, moe_norm, w_router, b_router,
              w_up, b_up, w_down, b_down, final_norm):
    B, S, D = x.shape
    h = x
    for l in range(DEPTH):
        u = rms_norm(h, norm_mix[l])
        proj = u @ w_in[l]
        q, k, v, p = jnp.split(proj, [SB_WIDTH, 2 * SB_WIDTH, 3 * SB_WIDTH], axis=-1)

        def heads(t):
            return t.reshape(B, S, SB_HEADS, SB_HEAD_DIM).transpose(0, 2, 1, 3)

        o_a = stick_breaking_attention(heads(q), heads(k), heads(v))
        o_a = o_a.transpose(0, 2, 1, 3).reshape(B, S, SB_WIDTH)
        o_b = multiscale_pool(p, pool_lin[l], pool_scale[l])

        y_a = o_a @ w_branch_a[l]
        y_b = o_b @ w_branch_b[l]
        g = jax.nn.sigmoid(u @ w_gate[l] + b_gate[l]).reshape(B, S, N_BRANCH, D)
        merged = g[:, :, 0] * y_a + g[:, :, 1] * y_b
        h = h + merged @ w_mix_out[l]

        h = h + memory_cross_attention(rms_norm(h, xa_norm[l]), rms_norm(mem, mem_norm[l]),
                                       xa_wq[l], xa_wkv[l], xa_wo[l])

        h = h + moe_ffn(rms_norm(h, moe_norm[l]), w_router[l], b_router[l],
                        w_up[l], b_up[l], w_down[l], b_down[l])
    return rms_norm(h, final_norm)
```

```python
import functools

import jax
import jax.numpy as jnp
from jax import lax
from jax.experimental import pallas as pl
from jax.experimental.pallas import tpu as pltpu

F32 = jnp.float32
BF16 = jnp.bfloat16
U32 = jnp.uint32
I32 = jnp.int32

RMS_EPS = 1e-5

SB_HEADS = 8
SB_HEAD_DIM = 128
SB_WIDTH = SB_HEADS * SB_HEAD_DIM
POOL_WINDOWS = (2, 4, 8, 16)
POOL_GROUP_DIM = 256
POOL_WIDTH = len(POOL_WINDOWS) * POOL_GROUP_DIM
POOL_HALO = 16
XA_HEADS = 4
XA_HEAD_DIM = 128
XA_WIDTH = XA_HEADS * XA_HEAD_DIM
N_EXPERTS = 32
TOP_K = 4
SWIGLU_ALPHA = 1.702
SWIGLU_LIMIT = 7.0

VMEM_LIMIT_BYTES = 56 * 1024 * 1024

SB_DEAD_LOG = 110.0

EXPERT_ROWS = 1024
EXPERT_SUB = 256
EXPERT_FT = 256


def _params(*sem):
    return pltpu.CompilerParams(dimension_semantics=sem, vmem_limit_bytes=VMEM_LIMIT_BYTES)


def _rms(x, gain):
    return x * lax.rsqrt(jnp.mean(x * x, axis=-1, keepdims=True) + RMS_EPS) * gain


def _linear_body(*refs, rms, has_scale, has_bias, act):
    refs = list(refs)
    a_ref = refs.pop(0)
    g_ref = refs.pop(0) if rms else None
    w_ref = refs.pop(0)
    s_ref = refs.pop(0) if has_scale else None
    b_ref = refs.pop(0) if has_bias else None
    o_ref, abf_ref = refs

    @pl.when(pl.program_id(1) == 0)
    def _():
        a = a_ref[...].astype(F32)
        if rms:
            a = _rms(a, g_ref[...])
        abf_ref[...] = a.astype(BF16)

    acc = jnp.dot(abf_ref[...], w_ref[...], preferred_element_type=F32)
    if has_scale:
        acc = acc * s_ref[...]
    if has_bias:
        acc = acc + b_ref[...]
    if act == "sigmoid":
        acc = jax.nn.sigmoid(acc)
    o_ref[...] = acc.astype(o_ref.dtype)


def _linear(a, w, *, gain=None, col_scale=None, bias=None, act=None, out_dtype=F32, tm=512, tn=1024):
    m, k = a.shape
    n = w.shape[1]
    tm, tn = min(tm, m), min(tn, n)
    assert m % tm == 0 and n % tn == 0
    args, specs = [a], [pl.BlockSpec((tm, k), lambda i, j: (i, 0))]
    if gain is not None:
        args.append(gain.reshape(1, k).astype(F32))
        specs.append(pl.BlockSpec((1, k), lambda i, j: (0, 0)))
    args.append(w)
    specs.append(pl.BlockSpec((k, tn), lambda i, j: (0, j)))
    for vec in (col_scale, bias):
        if vec is not None:
            args.append(vec.reshape(1, n).astype(F32))
            specs.append(pl.BlockSpec((1, tn), lambda i, j: (0, j)))
    body = functools.partial(_linear_body, rms=gain is not None, has_scale=col_scale is not None,
                             has_bias=bias is not None, act=act)
    return pl.pallas_call(
        body,
        grid=(m // tm, n // tn),
        in_specs=specs,
        out_specs=pl.BlockSpec((tm, tn), lambda i, j: (i, j)),
        out_shape=jax.ShapeDtypeStruct((m, n), out_dtype),
        scratch_shapes=[pltpu.VMEM((tm, k), BF16)],
        compiler_params=_params("parallel", "arbitrary"),
    )(*args)


def _sb_attn_body(q_ref, k_ref, v_ref, o_ref, acc_ref, run_ref, *, tq, tk):
    i = pl.program_id(1)
    q = q_ref[...]
    row = lax.broadcasted_iota(I32, (tq, tk), 0)
    col = lax.broadcasted_iota(I32, (tq, tk), 1)
    later_keys = (lax.broadcasted_iota(I32, (tk, tk), 0) > lax.broadcasted_iota(I32, (tk, tk), 1)).astype(BF16)

    acc_ref[...] = jnp.zeros_like(acc_ref)
    run_ref[...] = jnp.zeros_like(run_ref)

    def tile(kb):
        start = pl.multiple_of(kb * tk, tk)
        k = k_ref[pl.ds(start, tk), :]
        v = v_ref[pl.ds(start, tk), :]
        z = lax.dot_general(q, k, (((1,), (1,)), ((), ())), preferred_element_type=F32)
        softplus = jnp.maximum(z, 0.0) + jnp.log(1.0 + jnp.exp(-jnp.abs(z)))
        causal = (start + col) < (i * tq + row)
        neg_log_keep = jnp.where(causal, softplus, 0.0)
        hi = neg_log_keep.astype(BF16)
        lo = (neg_log_keep - hi.astype(F32)).astype(BF16)
        later = (jnp.dot(hi, later_keys, preferred_element_type=F32)
                 + jnp.dot(lo, later_keys, preferred_element_type=F32))
        run = run_ref[...]
        w = jnp.where(causal, jnp.exp(z - softplus - later - run), 0.0)
        acc_ref[...] += jnp.dot(w.astype(BF16), v, preferred_element_type=F32)
        run = run + jnp.sum(neg_log_keep, axis=1, keepdims=True)
        run_ref[...] = run
        return jnp.min(run)

    def cond(carry):
        kb, run_min = carry
        return jnp.logical_and(kb >= 0, run_min < SB_DEAD_LOG)

    def step(carry):
        kb, _ = carry
        return kb - 1, tile(kb)

    lax.while_loop(cond, step, ((i * tq) // tk, jnp.float32(0.0)))
    o_ref[...] = acc_ref[...].astype(o_ref.dtype)


def _sb_attention(qkv, *, heads, tq=256, tk=256):
    s = qkv.shape[0]
    dh = SB_HEAD_DIM
    tq, tk = min(tq, s), min(tk, s)
    assert tq == tk and s % tq == 0
    return pl.pallas_call(
        functools.partial(_sb_attn_body, tq=tq, tk=tk),
        grid=(heads, s // tq),
        in_specs=[
            pl.BlockSpec((tq, dh), lambda h, i: (i, h)),
            pl.BlockSpec((s, dh), lambda h, i: (0, heads + h)),
            pl.BlockSpec((s, dh), lambda h, i: (0, 2 * heads + h)),
        ],
        out_specs=pl.BlockSpec((tq, dh), lambda h, i: (i, h)),
        out_shape=jax.ShapeDtypeStruct((s, heads * dh), BF16),
        scratch_shapes=[pltpu.VMEM((tq, dh), F32), pltpu.VMEM((tq, 1), F32)],
        compiler_params=_params("parallel", "arbitrary"),
    )(qkv, qkv, qkv)


def _pool_body(p_ref, halo_ref, lin_ref, scale_ref, o_ref, *, ts):
    i = pl.program_id(0)
    cur = p_ref[...]
    halo = jnp.where(i > 0, halo_ref[...], 0.0)
    ext = jnp.concatenate([halo, cur], axis=0)
    pos = i * ts + lax.broadcasted_iota(I32, (ts, 1), 0)
    c = POOL_GROUP_DIM
    for g, window in enumerate(POOL_WINDOWS):
        total = ext[:, g * c:(g + 1) * c]
        span = 1
        while span < window:
            total = total + pltpu.roll(total, span, 0)
            span *= 2
        total = total[POOL_HALO:]
        count = jnp.minimum(pos + 1, window).astype(F32)
        pooled = total / count - cur[:, g * c:(g + 1) * c]
        mixed = jnp.dot(pooled.astype(BF16), lin_ref[g], preferred_element_type=F32)
        o_ref[:, g * c:(g + 1) * c] = (mixed * scale_ref[:, g * c:(g + 1) * c]).astype(o_ref.dtype)


def _pool(p, pool_lin, pool_scale, *, ts=512):
    s, width = p.shape
    ts = min(ts, s)
    assert s % ts == 0 and ts % POOL_HALO == 0
    per = ts // POOL_HALO
    groups = len(POOL_WINDOWS)
    return pl.pallas_call(
        functools.partial(_pool_body, ts=ts),
        grid=(s // ts,),
        in_specs=[
            pl.BlockSpec((ts, width), lambda i: (i, 0)),
            pl.BlockSpec((POOL_HALO, width), lambda i: (jnp.maximum(i * per - 1, 0), 0)),
            pl.BlockSpec((groups, POOL_GROUP_DIM, POOL_GROUP_DIM), lambda i: (0, 0, 0)),
            pl.BlockSpec((1, width), lambda i: (0, 0)),
        ],
        out_specs=pl.BlockSpec((ts, width), lambda i: (i, 0)),
        out_shape=jax.ShapeDtypeStruct((s, width), BF16),
        compiler_params=_params("parallel"),
    )(p, p, pool_lin, pool_scale.reshape(1, width).astype(F32))


def _merge_body(oa_ref, ob_ref, ga_ref, gb_ref, h_ref, wa_ref, wb_ref, wm_ref, o_ref):
    ya = jnp.dot(oa_ref[...], wa_ref[...], preferred_element_type=F32)
    yb = jnp.dot(ob_ref[...], wb_ref[...], preferred_element_type=F32)
    merged = ga_ref[...].astype(F32) * ya + gb_ref[...].astype(F32) * yb
    o_ref[...] = h_ref[...] + jnp.dot(merged.astype(BF16), wm_ref[...], preferred_element_type=F32)


def _const_spec(shape):
    return pl.BlockSpec(shape, lambda i: (0,) * len(shape), pipeline_mode=pl.Buffered(1))


def _merge(o_a, o_b, gates, h, w_a, w_b, w_mix, *, tm=256):
    s, d = h.shape
    tm = min(tm, s)
    return pl.pallas_call(
        _merge_body,
        grid=(s // tm,),
        in_specs=[
            pl.BlockSpec((tm, o_a.shape[1]), lambda i: (i, 0)),
            pl.BlockSpec((tm, o_b.shape[1]), lambda i: (i, 0)),
            pl.BlockSpec((tm, d), lambda i: (i, 0)),
            pl.BlockSpec((tm, d), lambda i: (i, 1)),
            pl.BlockSpec((tm, d), lambda i: (i, 0)),
            _const_spec(w_a.shape), _const_spec(w_b.shape), _const_spec(w_mix.shape),
        ],
        out_specs=pl.BlockSpec((tm, d), lambda i: (i, 0)),
        out_shape=jax.ShapeDtypeStruct((s, d), F32),
        compiler_params=_params("parallel"),
    )(o_a, o_b, gates, gates, h, w_a, w_b, w_mix)


def _pack_bf16_pairs(x):
    c = x.shape[1] // 2
    hi = pltpu.bitcast(x[:, :c].astype(BF16).astype(F32), U32)
    lo = pltpu.bitcast(x[:, c:].astype(BF16).astype(F32), U32)
    return hi | (lo >> 16)


def _unpack_bf16_pairs(u):
    hi = pltpu.bitcast(u & jnp.uint32(0xFFFF0000), F32).astype(BF16)
    lo = pltpu.bitcast(u << 16, F32).astype(BF16)
    return jnp.concatenate([hi, lo], axis=1)


def _xattn_router_body(h_ref, xg_ref, wq_ref, k_ref, v_ref, wo_ref, mg_ref, wr_ref, br_ref,
                       h2_ref, tok_ref, idx_ref, gate_ref):
    h = h_ref[...]
    u = _rms(h, xg_ref[...]).astype(BF16)
    q = (jnp.dot(u, wq_ref[...], preferred_element_type=F32) * (XA_HEAD_DIM ** -0.5)).astype(BF16)
    heads = []
    for hd in range(XA_HEADS):
        sl = slice(hd * XA_HEAD_DIM, (hd + 1) * XA_HEAD_DIM)
        s = lax.dot_general(q[:, sl], k_ref[:, sl], (((1,), (1,)), ((), ())), preferred_element_type=F32)
        p = jnp.exp(s - jnp.max(s, axis=-1, keepdims=True))
        o = jnp.dot(p.astype(BF16), v_ref[:, sl], preferred_element_type=F32)
        heads.append(o / jnp.sum(p, axis=-1, keepdims=True))
    o = jnp.concatenate(heads, axis=1).astype(BF16)
    h2 = h + jnp.dot(o, wo_ref[...], preferred_element_type=F32)
    h2_ref[...] = h2

    t = _rms(h2, mg_ref[...])
    tok_ref[...] = _pack_bf16_pairs(t)
    logits = jnp.dot(t, wr_ref[...], preferred_element_type=F32, precision=lax.Precision.HIGHEST) + br_ref[...]
    n_e = logits.shape[1]
    lane = lax.broadcasted_iota(I32, logits.shape, 1).astype(F32)
    vals, idxs = [], []
    for _ in range(TOP_K):
        best = jnp.max(logits, axis=-1, keepdims=True)
        arg = jnp.min(jnp.where(logits == best, lane, float(n_e)), axis=-1, keepdims=True)
        vals.append(best)
        idxs.append(arg)
        logits = jnp.where(lane == arg, -jnp.inf, logits)
    e = jnp.exp(jnp.concatenate(vals, axis=1) - vals[0])
    gate_ref[...] = e / jnp.sum(e, axis=-1, keepdims=True)
    idx_ref[...] = jnp.concatenate(idxs, axis=1).astype(I32)


def _xattn_router(h, xa_gain, wq, kx, vx, wo, moe_gain, w_router, b_router, *, tm=256):
    s, d = h.shape
    tm = min(tm, s)
    n_e = w_router.shape[1]
    row = lambda width: pl.BlockSpec((tm, width), lambda i: (i, 0))
    return pl.pallas_call(
        _xattn_router_body,
        grid=(s // tm,),
        in_specs=[
            row(d), _const_spec((1, d)), _const_spec(wq.shape), _const_spec(kx.shape), _const_spec(vx.shape),
            _const_spec(wo.shape), _const_spec((1, d)), _const_spec(w_router.shape), _const_spec((1, n_e)),
        ],
        out_specs=[row(d), row(d // 2), row(TOP_K), row(TOP_K)],
        out_shape=[
            jax.ShapeDtypeStruct((s, d), F32),
            jax.ShapeDtypeStruct((s, d // 2), U32),
            jax.ShapeDtypeStruct((s, TOP_K), I32),
            jax.ShapeDtypeStruct((s, TOP_K), F32),
        ],
        compiler_params=_params("parallel"),
    )(h, xa_gain.reshape(1, d), wq, kx, vx, wo, moe_gain.reshape(1, d), w_router, b_router.reshape(1, n_e))


def _dispatch_body(dest_ref, tok_hbm, zero_hbm, xs_hbm, sem, *, chunk):
    del zero_hbm
    base = pl.program_id(0) * chunk

    def row_copy(a):
        return pltpu.make_async_copy(tok_hbm.at[pl.ds(a // TOP_K, 1), :],
                                     xs_hbm.at[pl.ds(dest_ref[a], 1), :], sem)

    def issue(a, carry):
        row_copy(base + a).start()
        return carry

    def drain(a, carry):
        row_copy(base + a).wait()
        return carry

    lax.fori_loop(0, chunk, issue, 0)
    lax.fori_loop(0, chunk, drain, 0)


def _dispatch(tok, dest, n_rows, *, chunk=2048):
    n, c = tok.shape
    nk = dest.shape[0]
    chunk = min(chunk, nk)
    assert nk % chunk == 0
    return pl.pallas_call(
        functools.partial(_dispatch_body, chunk=chunk),
        grid_spec=pltpu.PrefetchScalarGridSpec(
            num_scalar_prefetch=1,
            grid=(nk // chunk,),
            in_specs=[pl.BlockSpec(memory_space=pl.ANY), pl.BlockSpec(memory_space=pl.ANY)],
            out_specs=pl.BlockSpec(memory_space=pl.ANY),
            scratch_shapes=[pltpu.SemaphoreType.DMA],
        ),
        out_shape=jax.ShapeDtypeStruct((n_rows, c), tok.dtype),
        input_output_aliases={2: 0},
        compiler_params=_params("arbitrary"),
    )(dest, tok, jnp.zeros((n_rows, c), tok.dtype))


def _expert_body(blk_e_ref, blk_src_ref, nsub_ref, x_ref, wg_ref, wl_ref, bg_ref, bl_ref, wd_ref, bd_ref,
                 o_ref, xb_ref, wgb_ref, wlb_ref, wdb_ref):
    del blk_e_ref, blk_src_ref
    b, j = pl.program_id(0), pl.program_id(1)
    nsub = nsub_ref[b]

    @pl.when(nsub > 0)
    def _():
        wgb_ref[...] = wg_ref[0].astype(BF16)
        wlb_ref[...] = wl_ref[0].astype(BF16)
        wdb_ref[...] = wd_ref[0].astype(BF16)

    for s in range(EXPERT_ROWS // EXPERT_SUB):
        rows = pl.ds(s * EXPERT_SUB, EXPERT_SUB)

        @pl.when(jnp.logical_and(s < nsub, j == 0))
        def _():
            xb_ref[rows, :] = _unpack_bf16_pairs(x_ref[rows, :])

        @pl.when(s < nsub)
        def _():
            x = xb_ref[rows, :]
            hg = jnp.dot(x, wgb_ref[...], preferred_element_type=F32) + bg_ref[0]
            hl = jnp.dot(x, wlb_ref[...], preferred_element_type=F32) + bl_ref[0]
            g = jnp.minimum(hg, SWIGLU_LIMIT)
            l = jnp.clip(hl, -SWIGLU_LIMIT, SWIGLU_LIMIT)
            act = g * jax.nn.sigmoid(SWIGLU_ALPHA * g) * (l + 1.0)
            y = jnp.dot(act.astype(BF16), wdb_ref[...], preferred_element_type=F32)

            @pl.when(j == 0)
            def _():
                o_ref[rows, :] = y + bd_ref[0]

            @pl.when(j > 0)
            def _():
                o_ref[rows, :] += y

        @pl.when(jnp.logical_and(jnp.logical_and(s >= nsub, nsub > 0), j == 0))
        def _():
            o_ref[rows, :] = jnp.zeros((EXPERT_SUB, o_ref.shape[1]), F32)


def _experts(xs, blk_expert, blk_src, blk_nsub, w_up, b_up, w_down, b_down):
    n_rows, half = xs.shape
    d = 2 * half
    n_e, _, two_f = w_up.shape
    f = two_f // 2
    ft = min(EXPERT_FT, f)
    nj = f // ft
    nb = n_rows // EXPERT_ROWS

    def step(j, ns, b):
        return jnp.where(ns[b] > 0, j, nj - 1)

    return pl.pallas_call(
        _expert_body,
        grid_spec=pltpu.PrefetchScalarGridSpec(
            num_scalar_prefetch=3,
            grid=(nb, nj),
            in_specs=[
                pl.BlockSpec((EXPERT_ROWS, half), lambda b, j, be, bs, ns: (bs[b], 0)),
                pl.BlockSpec((1, d, ft), lambda b, j, be, bs, ns: (be[b], 0, step(j, ns, b))),
                pl.BlockSpec((1, d, ft), lambda b, j, be, bs, ns: (be[b], 0, nj + step(j, ns, b))),
                pl.BlockSpec((1, 1, ft), lambda b, j, be, bs, ns: (be[b], 0, step(j, ns, b))),
                pl.BlockSpec((1, 1, ft), lambda b, j, be, bs, ns: (be[b], 0, nj + step(j, ns, b))),
                pl.BlockSpec((1, ft, d), lambda b, j, be, bs, ns: (be[b], step(j, ns, b), 0)),
                pl.BlockSpec((1, 1, d), lambda b, j, be, bs, ns: (be[b], 0, 0)),
            ],
            out_specs=pl.BlockSpec((EXPERT_ROWS, d), lambda b, j, be, bs, ns: (bs[b], 0)),
            scratch_shapes=[
                pltpu.VMEM((EXPERT_ROWS, d), BF16),
                pltpu.VMEM((d, ft), BF16), pltpu.VMEM((d, ft), BF16), pltpu.VMEM((ft, d), BF16),
            ],
        ),
        out_shape=jax.ShapeDtypeStruct((n_rows, d), F32),
        compiler_params=_params("arbitrary", "arbitrary"),
    )(blk_expert, blk_src, blk_nsub, xs, w_up, w_up, b_up.reshape(n_e, 1, two_f), b_up.reshape(n_e, 1, two_f),
      w_down, b_down.reshape(n_e, 1, d))


def _combine_body(dest_ref, ys_hbm, gate_ref, h_ref, fg_ref, o_ref, buf_ref, sem, *, tt):
    base = pl.program_id(0) * tt

    def row_copy(t, k):
        return pltpu.make_async_copy(ys_hbm.at[pl.ds(dest_ref[(base + t) * TOP_K + k], 1), :],
                                     buf_ref.at[k, pl.ds(t, 1), :], sem)

    def issue(t, carry):
        for k in range(TOP_K):
            row_copy(t, k).start()
        return carry

    def drain(t, carry):
        for k in range(TOP_K):
            row_copy(t, k).wait()
        return carry

    lax.fori_loop(0, tt, issue, 0)
    lax.fori_loop(0, tt, drain, 0)

    gate = gate_ref[...]
    y = h_ref[...]
    for k in range(TOP_K):
        y = y + gate[:, k:k + 1] * buf_ref[k]
    o_ref[...] = _rms(y, fg_ref[...])


def _combine(ys, dest, gate, h, final_gain, *, tt=128):
    n, d = h.shape
    tt = min(tt, n)
    return pl.pallas_call(
        functools.partial(_combine_body, tt=tt),
        grid_spec=pltpu.PrefetchScalarGridSpec(
            num_scalar_prefetch=1,
            grid=(n // tt,),
            in_specs=[
                pl.BlockSpec(memory_space=pl.ANY),
                pl.BlockSpec((tt, TOP_K), lambda i, dst: (i, 0)),
                pl.BlockSpec((tt, d), lambda i, dst: (i, 0)),
                pl.BlockSpec((1, d), lambda i, dst: (0, 0)),
            ],
            out_specs=pl.BlockSpec((tt, d), lambda i, dst: (i, 0)),
            scratch_shapes=[pltpu.VMEM((TOP_K, tt, d), F32), pltpu.SemaphoreType.DMA],
        ),
        out_shape=jax.ShapeDtypeStruct((n, d), F32),
        compiler_params=_params("arbitrary"),
    )(dest, ys, gate, h, final_gain.reshape(1, d))


def _routing_tables(top_idx, n_blocks):
    flat_e = top_idx.reshape(-1)
    onehot = (flat_e[:, None] == jnp.arange(N_EXPERTS, dtype=I32)[None, :]).astype(I32)
    seen = jnp.cumsum(onehot, axis=0)
    rank = jnp.take_along_axis(seen, flat_e[:, None], axis=1)[:, 0] - 1
    counts = seen[-1]
    blocks_per_e = (counts + EXPERT_ROWS - 1) // EXPERT_ROWS
    blk_end = jnp.cumsum(blocks_per_e)
    blk_start = blk_end - blocks_per_e
    dest = blk_start[flat_e] * EXPERT_ROWS + rank
    b = jnp.arange(n_blocks, dtype=I32)
    used = blk_end[-1]
    src = jnp.minimum(b, used - 1)
    blk_expert = jnp.minimum(jnp.sum(blk_end[None, :] <= src[:, None], axis=1), N_EXPERTS - 1).astype(I32)
    rows_here = jnp.clip(counts[blk_expert] - (src - blk_start[blk_expert]) * EXPERT_ROWS, 0, EXPERT_ROWS)
    nsub = jnp.where(b < used, (rows_here + EXPERT_SUB - 1) // EXPERT_SUB, 0)
    return dest.astype(I32), blk_expert, src.astype(I32), nsub.astype(I32)


def kernel(x, mem, norm_mix, w_in, pool_lin, pool_scale, w_branch_a, w_branch_b, w_gate, b_gate, w_mix_out,
           xa_norm, mem_norm, xa_wq, xa_wkv, xa_wo, moe_norm, w_router, b_router, w_up, b_up, w_down, b_down,
           final_norm):
    batch, s, d = x.shape
    assert batch == 1 and norm_mix.shape[0] == 1
    h = x.reshape(s, d)
    l = 0
    bf = lambda w: w.astype(BF16)

    qkv_w = 3 * SB_WIDTH
    q_scale = jnp.concatenate([jnp.full((SB_WIDTH,), SB_HEAD_DIM ** -0.5, F32), jnp.ones((2 * SB_WIDTH,), F32)])
    qkv = _linear(h, bf(w_in[l][:, :qkv_w]), gain=norm_mix[l], col_scale=q_scale, out_dtype=BF16)
    p = _linear(h, bf(w_in[l][:, qkv_w:]), gain=norm_mix[l], out_dtype=F32)
    gates = _linear(h, bf(w_gate[l]), gain=norm_mix[l], bias=b_gate[l], act="sigmoid", out_dtype=BF16)
    o_a = _sb_attention(qkv, heads=SB_HEADS)
    o_b = _pool(p, bf(pool_lin[l]), pool_scale[l])
    h = _merge(o_a, o_b, gates, h, bf(w_branch_a[l]), bf(w_branch_b[l]), bf(w_mix_out[l]))

    kv = _linear(mem.reshape(mem.shape[1], d), bf(xa_wkv[l]), gain=mem_norm[l], out_dtype=BF16)
    h, tok, top_idx, gate = _xattn_router(h, xa_norm[l], bf(xa_wq[l]), kv[:, :XA_WIDTH], kv[:, XA_WIDTH:],
                                          bf(xa_wo[l]), moe_norm[l], w_router[l], b_router[l])

    n_blocks = (s * TOP_K) // EXPERT_ROWS + N_EXPERTS
    dest, blk_expert, blk_src, blk_nsub = _routing_tables(top_idx, n_blocks)
    xs = _dispatch(tok, dest, n_blocks * EXPERT_ROWS)
    ys = _experts(xs, blk_expert, blk_src, blk_nsub, w_up[l], b_up[l], w_down[l], b_down[l])
    out = _combine(ys, dest, gate, h, final_norm)
    return out.reshape(batch, s, d)
```

```python
import functools

import jax
import jax.numpy as jnp
from jax import lax
from jax.experimental import pallas as pl
from jax.experimental.pallas import tpu as pltpu

F32 = jnp.float32
BF16 = jnp.bfloat16
U32 = jnp.uint32
I32 = jnp.int32

RMS_EPS = 1e-5

SB_HEADS = 8
SB_HEAD_DIM = 128
SB_WIDTH = SB_HEADS * SB_HEAD_DIM
POOL_WINDOWS = (2, 4, 8, 16)
POOL_GROUP_DIM = 256
POOL_WIDTH = len(POOL_WINDOWS) * POOL_GROUP_DIM
POOL_HALO = 16
XA_HEADS = 4
XA_HEAD_DIM = 128
XA_WIDTH = XA_HEADS * XA_HEAD_DIM
N_EXPERTS = 32
TOP_K = 4
SWIGLU_ALPHA = 1.702
SWIGLU_LIMIT = 7.0

VMEM_LIMIT_BYTES = 56 * 1024 * 1024

SB_DEAD_LOG = 110.0

EXPERT_ROWS = 1280
EXPERT_GRAIN = 128
GATHER_ROWS = 256
EXPERT_FT = 256
EXPERT_DOWN_CHUNK = 512


def _params(*sem):
    return pltpu.CompilerParams(dimension_semantics=sem, vmem_limit_bytes=VMEM_LIMIT_BYTES)


def _rms(x, gain):
    return x * lax.rsqrt(jnp.mean(x * x, axis=-1, keepdims=True) + RMS_EPS) * gain


def _linear_body(*refs, rms, has_scale, has_bias, act):
    refs = list(refs)
    a_ref = refs.pop(0)
    g_ref = refs.pop(0) if rms else None
    w_ref = refs.pop(0)
    s_ref = refs.pop(0) if has_scale else None
    b_ref = refs.pop(0) if has_bias else None
    o_ref, abf_ref = refs

    @pl.when(pl.program_id(1) == 0)
    def _():
        a = a_ref[...].astype(F32)
        if rms:
            a = _rms(a, g_ref[...])
        abf_ref[...] = a.astype(BF16)

    acc = jnp.dot(abf_ref[...], w_ref[...], preferred_element_type=F32)
    if has_scale:
        acc = acc * s_ref[...]
    if has_bias:
        acc = acc + b_ref[...]
    if act == "sigmoid":
        acc = jax.nn.sigmoid(acc)
    o_ref[...] = acc.astype(o_ref.dtype)


def _linear(a, w, *, gain=None, col_scale=None, bias=None, act=None, out_dtype=F32, tm=1024, tn=1024):
    m, k = a.shape
    n = w.shape[1]
    tm, tn = min(tm, m), min(tn, n)
    assert m % tm == 0 and n % tn == 0
    args, specs = [a], [pl.BlockSpec((tm, k), lambda i, j: (i, 0))]
    if gain is not None:
        args.append(gain.reshape(1, k).astype(F32))
        specs.append(pl.BlockSpec((1, k), lambda i, j: (0, 0)))
    args.append(w)
    specs.append(pl.BlockSpec((k, tn), lambda i, j: (0, j)))
    for vec in (col_scale, bias):
        if vec is not None:
            args.append(vec.reshape(1, n).astype(F32))
            specs.append(pl.BlockSpec((1, tn), lambda i, j: (0, j)))
    body = functools.partial(_linear_body, rms=gain is not None, has_scale=col_scale is not None,
                             has_bias=bias is not None, act=act)
    return pl.pallas_call(
        body,
        grid=(m // tm, n // tn),
        in_specs=specs,
        out_specs=pl.BlockSpec((tm, tn), lambda i, j: (i, j)),
        out_shape=jax.ShapeDtypeStruct((m, n), out_dtype),
        scratch_shapes=[pltpu.VMEM((tm, k), BF16)],
        compiler_params=_params("parallel", "arbitrary"),
    )(*args)


def _sb_attn_body(q_ref, k_ref, v_ref, o_ref, acc_ref, run_ref, *, tq, tk):
    i = pl.program_id(1)
    q = q_ref[...]
    row = lax.broadcasted_iota(I32, (tq, tk), 0)
    col = lax.broadcasted_iota(I32, (tq, tk), 1)
    later_keys = (lax.broadcasted_iota(I32, (tk, tk), 0) > lax.broadcasted_iota(I32, (tk, tk), 1)).astype(BF16)

    acc_ref[...] = jnp.zeros_like(acc_ref)
    run_ref[...] = jnp.zeros_like(run_ref)

    def tile(kb):
        start = pl.multiple_of(kb * tk, tk)
        k = k_ref[pl.ds(start, tk), :]
        v = v_ref[pl.ds(start, tk), :]
        z = lax.dot_general(q, k, (((1,), (1,)), ((), ())), preferred_element_type=F32)
        softplus = jnp.maximum(z, 0.0) + jnp.log(1.0 + jnp.exp(-jnp.abs(z)))
        causal = (start + col) < (i * tq + row)
        neg_log_keep = jnp.where(causal, softplus, 0.0)
        hi = neg_log_keep.astype(BF16)
        lo = (neg_log_keep - hi.astype(F32)).astype(BF16)
        later = (jnp.dot(hi, later_keys, preferred_element_type=F32)
                 + jnp.dot(lo, later_keys, preferred_element_type=F32))
        run = run_ref[...]
        w = jnp.where(causal, jnp.exp(z - softplus - later - run), 0.0)
        acc_ref[...] += jnp.dot(w.astype(BF16), v, preferred_element_type=F32)
        run = run + jnp.sum(neg_log_keep, axis=1, keepdims=True)
        run_ref[...] = run
        return jnp.min(run)

    def cond(carry):
        kb, run_min = carry
        return jnp.logical_and(kb >= 0, run_min < SB_DEAD_LOG)

    def step(carry):
        kb, _ = carry
        return kb - 1, tile(kb)

    lax.while_loop(cond, step, ((i * tq) // tk, jnp.float32(0.0)))
    o_ref[...] = acc_ref[...].astype(o_ref.dtype)


def _sb_attention(qkv, *, heads, tq=256, tk=256):
    s = qkv.shape[0]
    dh = SB_HEAD_DIM
    tq, tk = min(tq, s), min(tk, s)
    assert tq == tk and s % tq == 0
    return pl.pallas_call(
        functools.partial(_sb_attn_body, tq=tq, tk=tk),
        grid=(heads, s // tq),
        in_specs=[
            pl.BlockSpec((tq, dh), lambda h, i: (i, h)),
            pl.BlockSpec((s, dh), lambda h, i: (0, heads + h)),
            pl.BlockSpec((s, dh), lambda h, i: (0, 2 * heads + h)),
        ],
        out_specs=pl.BlockSpec((tq, dh), lambda h, i: (i, h)),
        out_shape=jax.ShapeDtypeStruct((s, heads * dh), BF16),
        scratch_shapes=[pltpu.VMEM((tq, dh), F32), pltpu.VMEM((tq, 1), F32)],
        compiler_params=_params("parallel", "arbitrary"),
    )(qkv, qkv, qkv)


def _pool_body(p_ref, halo_ref, lin_ref, scale_ref, o_ref, *, ts):
    i = pl.program_id(0)
    cur = p_ref[...]
    halo = jnp.where(i > 0, halo_ref[...], 0.0)
    ext = jnp.concatenate([halo, cur], axis=0)
    pos = i * ts + lax.broadcasted_iota(I32, (ts, 1), 0)
    c = POOL_GROUP_DIM
    for g, window in enumerate(POOL_WINDOWS):
        total = ext[:, g * c:(g + 1) * c]
        span = 1
        while span < window:
            total = total + pltpu.roll(total, span, 0)
            span *= 2
        total = total[POOL_HALO:]
        count = jnp.minimum(pos + 1, window).astype(F32)
        pooled = total / count - cur[:, g * c:(g + 1) * c]
        mixed = jnp.dot(pooled.astype(BF16), lin_ref[g], preferred_element_type=F32)
        o_ref[:, g * c:(g + 1) * c] = (mixed * scale_ref[:, g * c:(g + 1) * c]).astype(o_ref.dtype)


def _pool(p, pool_lin, pool_scale, *, ts=512):
    s, width = p.shape
    ts = min(ts, s)
    assert s % ts == 0 and ts % POOL_HALO == 0
    per = ts // POOL_HALO
    groups = len(POOL_WINDOWS)
    return pl.pallas_call(
        functools.partial(_pool_body, ts=ts),
        grid=(s // ts,),
        in_specs=[
            pl.BlockSpec((ts, width), lambda i: (i, 0)),
            pl.BlockSpec((POOL_HALO, width), lambda i: (jnp.maximum(i * per - 1, 0), 0)),
            pl.BlockSpec((groups, POOL_GROUP_DIM, POOL_GROUP_DIM), lambda i: (0, 0, 0)),
            pl.BlockSpec((1, width), lambda i: (0, 0)),
        ],
        out_specs=pl.BlockSpec((ts, width), lambda i: (i, 0)),
        out_shape=jax.ShapeDtypeStruct((s, width), BF16),
        compiler_params=_params("parallel"),
    )(p, p, pool_lin, pool_scale.reshape(1, width).astype(F32))


def _merge_body(oa_ref, ob_ref, ga_ref, gb_ref, h_ref, wa_ref, wb_ref, wm_ref, o_ref):
    ya = jnp.dot(oa_ref[...], wa_ref[...], preferred_element_type=F32)
    yb = jnp.dot(ob_ref[...], wb_ref[...], preferred_element_type=F32)
    merged = ga_ref[...].astype(F32) * ya + gb_ref[...].astype(F32) * yb
    o_ref[...] = h_ref[...] + jnp.dot(merged.astype(BF16), wm_ref[...], preferred_element_type=F32)


def _const_spec(shape):
    return pl.BlockSpec(shape, lambda i: (0,) * len(shape), pipeline_mode=pl.Buffered(1))


def _merge(o_a, o_b, gates, h, w_a, w_b, w_mix, *, tm=256):
    s, d = h.shape
    tm = min(tm, s)
    return pl.pallas_call(
        _merge_body,
        grid=(s // tm,),
        in_specs=[
            pl.BlockSpec((tm, o_a.shape[1]), lambda i: (i, 0)),
            pl.BlockSpec((tm, o_b.shape[1]), lambda i: (i, 0)),
            pl.BlockSpec((tm, d), lambda i: (i, 0)),
            pl.BlockSpec((tm, d), lambda i: (i, 1)),
            pl.BlockSpec((tm, d), lambda i: (i, 0)),
            _const_spec(w_a.shape), _const_spec(w_b.shape), _const_spec(w_mix.shape),
        ],
        out_specs=pl.BlockSpec((tm, d), lambda i: (i, 0)),
        out_shape=jax.ShapeDtypeStruct((s, d), F32),
        compiler_params=_params("parallel"),
    )(o_a, o_b, gates, gates, h, w_a, w_b, w_mix)


def _xattn_router_body(h_ref, xg_ref, wq_ref, k_ref, v_ref, wo_ref, mg_ref, wr_ref, br_ref,
                       h2_ref, idx_ref, gate_ref):
    h = h_ref[...]
    u = _rms(h, xg_ref[...]).astype(BF16)
    q = (jnp.dot(u, wq_ref[...], preferred_element_type=F32) * (XA_HEAD_DIM ** -0.5)).astype(BF16)
    heads = []
    for hd in range(XA_HEADS):
        sl = slice(hd * XA_HEAD_DIM, (hd + 1) * XA_HEAD_DIM)
        s = lax.dot_general(q[:, sl], k_ref[:, sl], (((1,), (1,)), ((), ())), preferred_element_type=F32)
        p = jnp.exp(s - jnp.max(s, axis=-1, keepdims=True))
        o = jnp.dot(p.astype(BF16), v_ref[:, sl], preferred_element_type=F32)
        heads.append(o / jnp.sum(p, axis=-1, keepdims=True))
    o = jnp.concatenate(heads, axis=1).astype(BF16)
    h2 = h + jnp.dot(o, wo_ref[...], preferred_element_type=F32)
    h2_ref[...] = h2

    t = _rms(h2, mg_ref[...])
    logits = jnp.dot(t, wr_ref[...], preferred_element_type=F32, precision=lax.Precision.HIGHEST) + br_ref[...]
    n_e = logits.shape[1]
    lane = lax.broadcasted_iota(I32, logits.shape, 1).astype(F32)
    vals, idxs = [], []
    for _ in range(TOP_K):
        best = jnp.max(logits, axis=-1, keepdims=True)
        arg = jnp.min(jnp.where(logits == best, lane, float(n_e)), axis=-1, keepdims=True)
        vals.append(best)
        idxs.append(arg)
        logits = jnp.where(lane == arg, -jnp.inf, logits)
    e = jnp.exp(jnp.concatenate(vals, axis=1) - vals[0])
    gate_ref[...] = e / jnp.sum(e, axis=-1, keepdims=True)
    idx_ref[...] = jnp.concatenate(idxs, axis=1).astype(I32)


def _xattn_router(h, xa_gain, wq, kx, vx, wo, moe_gain, w_router, b_router, *, tm=256):
    s, d = h.shape
    tm = min(tm, s)
    n_e = w_router.shape[1]
    row = lambda width: pl.BlockSpec((tm, width), lambda i: (i, 0))
    return pl.pallas_call(
        _xattn_router_body,
        grid=(s // tm,),
        in_specs=[
            row(d), _const_spec((1, d)), _const_spec(wq.shape), _const_spec(kx.shape), _const_spec(vx.shape),
            _const_spec(wo.shape), _const_spec((1, d)), _const_spec(w_router.shape), _const_spec((1, n_e)),
        ],
        out_specs=[row(d), row(TOP_K), row(TOP_K)],
        out_shape=[
            jax.ShapeDtypeStruct((s, d), F32),
            jax.ShapeDtypeStruct((s, TOP_K), I32),
            jax.ShapeDtypeStruct((s, TOP_K), F32),
        ],
        compiler_params=_params("parallel"),
    )(h, xa_gain.reshape(1, d), wq, kx, vx, wo, moe_gain.reshape(1, d), w_router, b_router.reshape(1, n_e))


def _gather_body(row_tok_ref, sub_dst_ref, used_ref, h_hbm, gain_ref, o_ref, buf_ref, sem, *, rows):
    del sub_dst_ref
    g = pl.program_id(0)
    n_used = used_ref[0]

    def row_copy(step, r):
        slot = step % 2
        return pltpu.make_async_copy(h_hbm.at[pl.ds(row_tok_ref[step * rows + r], 1), :],
                                     buf_ref.at[slot, pl.ds(r, 1), :], sem.at[slot])

    @pl.when(g < n_used)
    def _():
        for r in range(rows):
            row_copy(g, r).start()

    @pl.when(jnp.logical_and(g >= 1, g <= n_used))
    def _():
        def drain(r, carry):
            row_copy(g - 1, r).wait()
            return carry

        lax.fori_loop(0, rows, drain, 0, unroll=8)
        o_ref[...] = _rms(buf_ref[(g - 1) % 2], gain_ref[...]).astype(o_ref.dtype)


def _gather_rows(h, gain, row_tok, sub_dst, n_used, n_rows):
    d = h.shape[1]
    rows = GATHER_ROWS
    return pl.pallas_call(
        functools.partial(_gather_body, rows=rows),
        grid_spec=pltpu.PrefetchScalarGridSpec(
            num_scalar_prefetch=3,
            grid=(sub_dst.shape[0] + 1,),
            in_specs=[pl.BlockSpec(memory_space=pl.ANY), pl.BlockSpec((1, d), lambda g, rt, sd, nu: (0, 0))],
            out_specs=pl.BlockSpec((rows, d), lambda g, rt, sd, nu: (sd[jnp.maximum(g - 1, 0)], 0)),
            scratch_shapes=[pltpu.VMEM((2, rows, d), F32), pltpu.SemaphoreType.DMA((2,))],
        ),
        out_shape=jax.ShapeDtypeStruct((n_rows, d), BF16),
        compiler_params=_params("arbitrary"),
    )(row_tok, sub_dst, n_used, h, gain.reshape(1, d))


def _expert_body(blk_e_ref, blk_src_ref, nsub_ref, x_ref, wg_ref, wl_ref, bg_ref, bl_ref, wd_ref, bd_ref,
                 o_ref, wgb_ref, wlb_ref, wdb_ref):
    del blk_e_ref, blk_src_ref
    b, j = pl.program_id(0), pl.program_id(1)
    nsub = nsub_ref[b]
    d = o_ref.shape[1]

    @pl.when(nsub > 0)
    def _():
        wgb_ref[...] = wg_ref[0].astype(BF16)
        wlb_ref[...] = wl_ref[0].astype(BF16)
        wdb_ref[...] = wd_ref[0].astype(BF16)

    @pl.when(jnp.logical_and(nsub > 0, j == 0))
    def _():
        o_ref[...] = jnp.broadcast_to(bd_ref[0], o_ref.shape)

    for m_sub in range(1, EXPERT_ROWS // EXPERT_GRAIN + 1):
        @pl.when(nsub == m_sub)
        def _():
            m = m_sub * EXPERT_GRAIN
            x = x_ref[0:m, :]
            hg = jnp.dot(x, wgb_ref[...], preferred_element_type=F32) + bg_ref[0]
            hl = jnp.dot(x, wlb_ref[...], preferred_element_type=F32) + bl_ref[0]
            g = jnp.minimum(hg, SWIGLU_LIMIT)
            l = jnp.clip(hl, -SWIGLU_LIMIT, SWIGLU_LIMIT)
            act = (g * jax.nn.sigmoid(SWIGLU_ALPHA * g) * (l + 1.0)).astype(BF16)
            for n in range(0, d, EXPERT_DOWN_CHUNK):
                cols = slice(n, min(n + EXPERT_DOWN_CHUNK, d))
                o_ref[0:m, cols] += jnp.dot(act, wdb_ref[:, cols], preferred_element_type=F32)


def _experts(xs, blk_expert, blk_src, blk_nsub, w_up, b_up, w_down, b_down):
    n_rows, d = xs.shape
    n_e, _, two_f = w_up.shape
    f = two_f // 2
    ft = min(EXPERT_FT, f)
    nj = f // ft
    nb = n_rows // EXPERT_ROWS

    def step(j, ns, b):
        return jnp.where(ns[b] > 0, j, nj - 1)

    return pl.pallas_call(
        _expert_body,
        grid_spec=pltpu.PrefetchScalarGridSpec(
            num_scalar_prefetch=3,
            grid=(nb, nj),
            in_specs=[
                pl.BlockSpec((EXPERT_ROWS, d), lambda b, j, be, bs, ns: (bs[b], 0)),
                pl.BlockSpec((1, d, ft), lambda b, j, be, bs, ns: (be[b], 0, step(j, ns, b))),
                pl.BlockSpec((1, d, ft), lambda b, j, be, bs, ns: (be[b], 0, nj + step(j, ns, b))),
                pl.BlockSpec((1, 1, ft), lambda b, j, be, bs, ns: (be[b], 0, step(j, ns, b))),
                pl.BlockSpec((1, 1, ft), lambda b, j, be, bs, ns: (be[b], 0, nj + step(j, ns, b))),
                pl.BlockSpec((1, ft, d), lambda b, j, be, bs, ns: (be[b], step(j, ns, b), 0)),
                pl.BlockSpec((1, 1, d), lambda b, j, be, bs, ns: (be[b], 0, 0)),
            ],
            out_specs=pl.BlockSpec((EXPERT_ROWS, d), lambda b, j, be, bs, ns: (bs[b], 0)),
            scratch_shapes=[pltpu.VMEM((d, ft), BF16), pltpu.VMEM((d, ft), BF16), pltpu.VMEM((ft, d), BF16)],
        ),
        out_shape=jax.ShapeDtypeStruct((n_rows, d), F32),
        compiler_params=_params("arbitrary", "arbitrary"),
    )(blk_expert, blk_src, blk_nsub, xs, w_up, w_up, b_up.reshape(n_e, 1, two_f), b_up.reshape(n_e, 1, two_f),
      w_down, b_down.reshape(n_e, 1, d))


def _combine_body(dest_ref, ys_hbm, gate_ref, h_ref, fg_ref, o_ref, buf_ref, sem, *, tt, n_tiles):
    i = pl.program_id(0)

    def row_copy(tile, t, k):
        slot = tile % 2
        return pltpu.make_async_copy(ys_hbm.at[pl.ds(dest_ref[(tile * tt + t) * TOP_K + k], 1), :],
                                     buf_ref.at[slot, k, pl.ds(t, 1), :], sem.at[slot])

    @pl.when(i < n_tiles)
    def _():
        for t in range(tt):
            for k in range(TOP_K):
                row_copy(i, t, k).start()

    @pl.when(i >= 1)
    def _():
        def drain(t, carry):
            for k in range(TOP_K):
                row_copy(i - 1, t, k).wait()
            return carry

        lax.fori_loop(0, tt, drain, 0, unroll=4)
        slot = (i - 1) % 2
        gate = gate_ref[...]
        y = h_ref[...]
        for k in range(TOP_K):
            y = y + gate[:, k:k + 1] * buf_ref[slot, k]
        o_ref[...] = _rms(y, fg_ref[...])


def _combine(ys, dest, gate, h, final_gain, *, tt=128):
    n, d = h.shape
    tt = min(tt, n)
    n_tiles = n // tt
    prev = lambda i, dst: (jnp.maximum(i - 1, 0), 0)
    return pl.pallas_call(
        functools.partial(_combine_body, tt=tt, n_tiles=n_tiles),
        grid_spec=pltpu.PrefetchScalarGridSpec(
            num_scalar_prefetch=1,
            grid=(n_tiles + 1,),
            in_specs=[
                pl.BlockSpec(memory_space=pl.ANY),
                pl.BlockSpec((tt, TOP_K), prev),
                pl.BlockSpec((tt, d), prev),
                pl.BlockSpec((1, d), lambda i, dst: (0, 0)),
            ],
            out_specs=pl.BlockSpec((tt, d), prev),
            scratch_shapes=[pltpu.VMEM((2, TOP_K, tt, d), F32), pltpu.SemaphoreType.DMA((2,))],
        ),
        out_shape=jax.ShapeDtypeStruct((n, d), F32),
        compiler_params=_params("arbitrary"),
    )(dest, ys, gate, h, final_gain.reshape(1, d))


def _routing_tables(top_idx, n_blocks, n_subs):
    assert EXPERT_ROWS % GATHER_ROWS == 0 and GATHER_ROWS % EXPERT_GRAIN == 0
    flat_e = top_idx.reshape(-1)
    nk = flat_e.shape[0]
    onehot = (flat_e[:, None] == jnp.arange(N_EXPERTS, dtype=I32)[None, :]).astype(I32)
    seen = jnp.cumsum(onehot, axis=0)
    rank = jnp.take_along_axis(seen, flat_e[:, None], axis=1)[:, 0] - 1
    counts = seen[-1]
    subs_per_block = EXPERT_ROWS // GATHER_ROWS

    blocks_per_e = (counts + EXPERT_ROWS - 1) // EXPERT_ROWS
    blk_end = jnp.cumsum(blocks_per_e)
    blk_start = blk_end - blocks_per_e
    dest = blk_start[flat_e] * EXPERT_ROWS + rank

    subs_per_e = (counts + GATHER_ROWS - 1) // GATHER_ROWS
    sub_end = jnp.cumsum(subs_per_e)
    sub_start = sub_end - subs_per_e
    compact = sub_start[flat_e] * GATHER_ROWS + rank
    row_tok = jnp.zeros((n_subs * GATHER_ROWS,), I32).at[compact].set(jnp.arange(nk, dtype=I32) // TOP_K)
    g_src = jnp.minimum(jnp.arange(n_subs, dtype=I32), sub_end[-1] - 1)
    g_expert = jnp.minimum(jnp.sum(sub_end[None, :] <= g_src[:, None], axis=1), N_EXPERTS - 1)
    sub_dst = blk_start[g_expert] * subs_per_block + (g_src - sub_start[g_expert])
    subs_used = sub_end[-1:]

    b = jnp.arange(n_blocks, dtype=I32)
    used = blk_end[-1]
    src = jnp.minimum(b, used - 1)
    blk_expert = jnp.minimum(jnp.sum(blk_end[None, :] <= src[:, None], axis=1), N_EXPERTS - 1).astype(I32)
    rows_here = jnp.clip(counts[blk_expert] - (src - blk_start[blk_expert]) * EXPERT_ROWS, 0, EXPERT_ROWS)
    grains = jnp.where(b < used, (rows_here + EXPERT_GRAIN - 1) // EXPERT_GRAIN, 0)
    i32 = lambda a: a.astype(I32)
    return i32(dest), row_tok, i32(sub_dst), i32(subs_used), blk_expert, i32(src), i32(grains)


def kernel(x, mem, norm_mix, w_in, pool_lin, pool_scale, w_branch_a, w_branch_b, w_gate, b_gate, w_mix_out,
           xa_norm, mem_norm, xa_wq, xa_wkv, xa_wo, moe_norm, w_router, b_router, w_up, b_up, w_down, b_down,
           final_norm):
    batch, s, d = x.shape
    assert batch == 1 and norm_mix.shape[0] == 1
    h = x.reshape(s, d)
    l = 0
    bf = lambda w: w.astype(BF16)

    qkv_w = 3 * SB_WIDTH
    q_scale = jnp.concatenate([jnp.full((SB_WIDTH,), SB_HEAD_DIM ** -0.5, F32), jnp.ones((2 * SB_WIDTH,), F32)])
    qkv = _linear(h, bf(w_in[l][:, :qkv_w]), gain=norm_mix[l], col_scale=q_scale, out_dtype=BF16)
    p = _linear(h, bf(w_in[l][:, qkv_w:]), gain=norm_mix[l], out_dtype=F32)
    gates = _linear(h, bf(w_gate[l]), gain=norm_mix[l], bias=b_gate[l], act="sigmoid", out_dtype=BF16)
    o_a = _sb_attention(qkv, heads=SB_HEADS)
    o_b = _pool(p, bf(pool_lin[l]), pool_scale[l])
    h = _merge(o_a, o_b, gates, h, bf(w_branch_a[l]), bf(w_branch_b[l]), bf(w_mix_out[l]))

    kv = _linear(mem.reshape(mem.shape[1], d), bf(xa_wkv[l]), gain=mem_norm[l], out_dtype=BF16)
    h, top_idx, gate = _xattn_router(h, xa_norm[l], bf(xa_wq[l]), kv[:, :XA_WIDTH], kv[:, XA_WIDTH:],
                                     bf(xa_wo[l]), moe_norm[l], w_router[l], b_router[l])

    n_blocks = (s * TOP_K) // EXPERT_ROWS + N_EXPERTS
    n_subs = (s * TOP_K) // GATHER_ROWS + N_EXPERTS
    dest, row_tok, sub_dst, subs_used, blk_expert, blk_src, blk_nsub = _routing_tables(top_idx, n_blocks, n_subs)
    xs = _gather_rows(h, moe_norm[l], row_tok, sub_dst, subs_used, n_blocks * EXPERT_ROWS)
    ys = _experts(xs, blk_expert, blk_src, blk_nsub, w_up[l], b_up[l], w_down[l], b_down[l])
    out = _combine(ys, dest, gate, h, final_norm)
    return out.reshape(batch, s, d)
```

```python
import functools

import jax
import jax.numpy as jnp
from jax import lax
from jax.experimental import pallas as pl
from jax.experimental.pallas import tpu as pltpu

F32 = jnp.float32
BF16 = jnp.bfloat16
U32 = jnp.uint32
I32 = jnp.int32

RMS_EPS = 1e-5

SB_HEADS = 8
SB_HEAD_DIM = 128
SB_WIDTH = SB_HEADS * SB_HEAD_DIM
POOL_WINDOWS = (2, 4, 8, 16)
POOL_GROUP_DIM = 256
POOL_WIDTH = len(POOL_WINDOWS) * POOL_GROUP_DIM
POOL_HALO = 16
XA_HEADS = 4
XA_HEAD_DIM = 128
XA_WIDTH = XA_HEADS * XA_HEAD_DIM
N_EXPERTS = 32
TOP_K = 4
SWIGLU_ALPHA = 1.702
SWIGLU_LIMIT = 7.0

VMEM_LIMIT_BYTES = 56 * 1024 * 1024

SB_DEAD_LOG = 110.0

EXPERT_ROWS = 1280
EXPERT_GRAIN = 128
GATHER_ROWS = 256
EXPERT_FT = 256
EXPERT_DOWN_CHUNK = 512


def _params(*sem):
    return pltpu.CompilerParams(dimension_semantics=sem, vmem_limit_bytes=VMEM_LIMIT_BYTES)


def _rms(x, gain):
    return x * lax.rsqrt(jnp.mean(x * x, axis=-1, keepdims=True) + RMS_EPS) * gain


def _linear_body(*refs, rms, has_scale, has_bias, act):
    refs = list(refs)
    a_ref = refs.pop(0)
    g_ref = refs.pop(0) if rms else None
    w_ref = refs.pop(0)
    s_ref = refs.pop(0) if has_scale else None
    b_ref = refs.pop(0) if has_bias else None
    o_ref, abf_ref = refs

    @pl.when(pl.program_id(1) == 0)
    def _():
        a = a_ref[...].astype(F32)
        if rms:
            a = _rms(a, g_ref[...])
        abf_ref[...] = a.astype(BF16)

    acc = jnp.dot(abf_ref[...], w_ref[...], preferred_element_type=F32)
    if has_scale:
        acc = acc * s_ref[...]
    if has_bias:
        acc = acc + b_ref[...]
    if act == "sigmoid":
        acc = jax.nn.sigmoid(acc)
    o_ref[...] = acc.astype(o_ref.dtype)


def _linear(a, w, *, gain=None, col_scale=None, bias=None, act=None, out_dtype=F32, tm=1024, tn=1024):
    m, k = a.shape
    n = w.shape[1]
    tm, tn = min(tm, m), min(tn, n)
    assert m % tm == 0 and n % tn == 0
    args, specs = [a], [pl.BlockSpec((tm, k), lambda i, j: (i, 0))]
    if gain is not None:
        args.append(gain.reshape(1, k).astype(F32))
        specs.append(pl.BlockSpec((1, k), lambda i, j: (0, 0)))
    args.append(w)
    specs.append(pl.BlockSpec((k, tn), lambda i, j: (0, j)))
    for vec in (col_scale, bias):
        if vec is not None:
            args.append(vec.reshape(1, n).astype(F32))
            specs.append(pl.BlockSpec((1, tn), lambda i, j: (0, j)))
    body = functools.partial(_linear_body, rms=gain is not None, has_scale=col_scale is not None,
                             has_bias=bias is not None, act=act)
    return pl.pallas_call(
        body,
        grid=(m // tm, n // tn),
        in_specs=specs,
        out_specs=pl.BlockSpec((tm, tn), lambda i, j: (i, j)),
        out_shape=jax.ShapeDtypeStruct((m, n), out_dtype),
        scratch_shapes=[pltpu.VMEM((tm, k), BF16)],
        compiler_params=_params("parallel", "arbitrary"),
    )(*args)


def _sb_attn_body(q_ref, k_ref, v_ref, o_ref, acc_ref, run_ref, *, t, group):
    i = pl.program_id(1)
    dh = SB_HEAD_DIM
    row = lax.broadcasted_iota(I32, (t, t), 0)
    col = lax.broadcasted_iota(I32, (t, t), 1)
    causal = col < row
    later_keys = (row > col).astype(BF16)
    later_keys = jnp.concatenate([later_keys, later_keys], axis=0)
    heads = range(group)
    head_cols = [slice(h * dh, (h + 1) * dh) for h in heads]

    def tile(kb, diagonal):
        start = pl.multiple_of(kb * t, t)
        z = [lax.dot_general(q_ref[:, c], k_ref[pl.ds(start, t), c], (((1,), (1,)), ((), ())),
                             preferred_element_type=F32) for c in head_cols]
        softplus = [jnp.maximum(x, 0.0) + jnp.log(1.0 + jnp.exp(-jnp.abs(x))) for x in z]
        neg_log_keep = [jnp.where(causal, x, 0.0) for x in softplus] if diagonal else softplus
        later = []
        for x in neg_log_keep:
            hi = x.astype(BF16)
            lo = (x - hi.astype(F32)).astype(BF16)
            later.append(jnp.dot(jnp.concatenate([hi, lo], axis=1), later_keys, preferred_element_type=F32))
        run_min = None
        for h in heads:
            if diagonal:
                w = jnp.where(causal, jnp.exp(z[h] - softplus[h] - later[h]), 0.0)
                run = jnp.sum(neg_log_keep[h], axis=1, keepdims=True)
            else:
                run = run_ref[h]
                w = jnp.exp(z[h] - softplus[h] - later[h] - run)
                run = run + jnp.sum(neg_log_keep[h], axis=1, keepdims=True)
            pv = jnp.dot(w.astype(BF16), v_ref[pl.ds(start, t), head_cols[h]], preferred_element_type=F32)
            if diagonal:
                acc_ref[h] = pv
            else:
                acc_ref[h] += pv
            run_ref[h] = run
            head_min = jnp.min(run)
            run_min = head_min if run_min is None else jnp.minimum(run_min, head_min)
        return run_min

    def cond(carry):
        kb, run_min = carry
        return jnp.logical_and(kb >= 0, run_min < SB_DEAD_LOG)

    def step(carry):
        kb, _ = carry
        return kb - 1, tile(kb, False)

    lax.while_loop(cond, step, (i - 1, tile(i, True)))
    for h in range(group):
        o_ref[:, h * dh:(h + 1) * dh] = acc_ref[h].astype(o_ref.dtype)


def _sb_attention(qkv, *, heads, t=256, group=4):
    s = qkv.shape[0]
    dh = SB_HEAD_DIM
    t, group = min(t, s), min(group, heads)
    assert s % t == 0 and heads % group == 0
    n_groups = heads // group
    width = group * dh
    return pl.pallas_call(
        functools.partial(_sb_attn_body, t=t, group=group),
        grid=(n_groups, s // t),
        in_specs=[
            pl.BlockSpec((t, width), lambda g, i: (i, g)),
            pl.BlockSpec((s, width), lambda g, i: (0, n_groups + g)),
            pl.BlockSpec((s, width), lambda g, i: (0, 2 * n_groups + g)),
        ],
        out_specs=pl.BlockSpec((t, width), lambda g, i: (i, g)),
        out_shape=jax.ShapeDtypeStruct((s, heads * dh), BF16),
        scratch_shapes=[pltpu.VMEM((group, t, dh), F32), pltpu.VMEM((group, t, 1), F32)],
        compiler_params=_params("parallel", "arbitrary"),
    )(qkv, qkv, qkv)


def _pool_body(p_ref, halo_ref, lin_ref, scale_ref, o_ref, *, ts):
    i = pl.program_id(0)
    cur = p_ref[...]
    halo = jnp.where(i > 0, halo_ref[...], 0.0)
    ext = jnp.concatenate([halo, cur], axis=0)
    pos = i * ts + lax.broadcasted_iota(I32, (ts, 1), 0)
    c = POOL_GROUP_DIM
    for g, window in enumerate(POOL_WINDOWS):
        total = ext[:, g * c:(g + 1) * c]
        span = 1
        while span < window:
            total = total + pltpu.roll(total, span, 0)
            span *= 2
        total = total[POOL_HALO:]
        count = jnp.minimum(pos + 1, window).astype(F32)
        pooled = total / count - cur[:, g * c:(g + 1) * c]
        mixed = jnp.dot(pooled.astype(BF16), lin_ref[g], preferred_element_type=F32)
        o_ref[:, g * c:(g + 1) * c] = (mixed * scale_ref[:, g * c:(g + 1) * c]).astype(o_ref.dtype)


def _pool(p, pool_lin, pool_scale, *, ts=512):
    s, width = p.shape
    ts = min(ts, s)
    assert s % ts == 0 and ts % POOL_HALO == 0
    per = ts // POOL_HALO
    groups = len(POOL_WINDOWS)
    return pl.pallas_call(
        functools.partial(_pool_body, ts=ts),
        grid=(s // ts,),
        in_specs=[
            pl.BlockSpec((ts, width), lambda i: (i, 0)),
            pl.BlockSpec((POOL_HALO, width), lambda i: (jnp.maximum(i * per - 1, 0), 0)),
            pl.BlockSpec((groups, POOL_GROUP_DIM, POOL_GROUP_DIM), lambda i: (0, 0, 0)),
            pl.BlockSpec((1, width), lambda i: (0, 0)),
        ],
        out_specs=pl.BlockSpec((ts, width), lambda i: (i, 0)),
        out_shape=jax.ShapeDtypeStruct((s, width), BF16),
        compiler_params=_params("parallel"),
    )(p, p, pool_lin, pool_scale.reshape(1, width).astype(F32))


def _merge_body(oa_ref, ob_ref, ga_ref, gb_ref, h_ref, wa_ref, wb_ref, wm_ref, o_ref):
    ya = jnp.dot(oa_ref[...], wa_ref[...], preferred_element_type=F32)
    yb = jnp.dot(ob_ref[...], wb_ref[...], preferred_element_type=F32)
    merged = ga_ref[...].astype(F32) * ya + gb_ref[...].astype(F32) * yb
    o_ref[...] = h_ref[...] + jnp.dot(merged.astype(BF16), wm_ref[...], preferred_element_type=F32)


def _const_spec(shape):
    return pl.BlockSpec(shape, lambda i: (0,) * len(shape), pipeline_mode=pl.Buffered(1))


def _merge(o_a, o_b, gates, h, w_a, w_b, w_mix, *, tm=256):
    s, d = h.shape
    tm = min(tm, s)
    return pl.pallas_call(
        _merge_body,
        grid=(s // tm,),
        in_specs=[
            pl.BlockSpec((tm, o_a.shape[1]), lambda i: (i, 0)),
            pl.BlockSpec((tm, o_b.shape[1]), lambda i: (i, 0)),
            pl.BlockSpec((tm, d), lambda i: (i, 0)),
            pl.BlockSpec((tm, d), lambda i: (i, 1)),
            pl.BlockSpec((tm, d), lambda i: (i, 0)),
            _const_spec(w_a.shape), _const_spec(w_b.shape), _const_spec(w_mix.shape),
        ],
        out_specs=pl.BlockSpec((tm, d), lambda i: (i, 0)),
        out_shape=jax.ShapeDtypeStruct((s, d), F32),
        compiler_params=_params("parallel"),
    )(o_a, o_b, gates, gates, h, w_a, w_b, w_mix)


def _xattn_router_body(h_ref, xg_ref, wq_ref, k_ref, v_ref, wo_ref, mg_ref, wr_ref, br_ref,
                       h2_ref, idx_ref, gate_ref):
    h = h_ref[...]
    u = _rms(h, xg_ref[...]).astype(BF16)
    q = (jnp.dot(u, wq_ref[...], preferred_element_type=F32) * (XA_HEAD_DIM ** -0.5)).astype(BF16)
    heads = []
    for hd in range(XA_HEADS):
        sl = slice(hd * XA_HEAD_DIM, (hd + 1) * XA_HEAD_DIM)
        s = lax.dot_general(q[:, sl], k_ref[:, sl], (((1,), (1,)), ((), ())), preferred_element_type=F32)
        p = jnp.exp(s - jnp.max(s, axis=-1, keepdims=True))
        o = jnp.dot(p.astype(BF16), v_ref[:, sl], preferred_element_type=F32)
        heads.append(o / jnp.sum(p, axis=-1, keepdims=True))
    o = jnp.concatenate(heads, axis=1).astype(BF16)
    h2 = h + jnp.dot(o, wo_ref[...], preferred_element_type=F32)
    h2_ref[...] = h2

    t = _rms(h2, mg_ref[...])
    t_hi = t.astype(BF16)
    t_lo = (t - t_hi.astype(F32)).astype(BF16)
    logits = (jnp.dot(t_hi, wr_ref[0], preferred_element_type=F32)
              + jnp.dot(t_hi, wr_ref[1], preferred_element_type=F32)
              + jnp.dot(t_lo, wr_ref[0], preferred_element_type=F32)) + br_ref[...]
    n_e = logits.shape[1]
    lane = lax.broadcasted_iota(I32, logits.shape, 1).astype(F32)
    vals, idxs = [], []
    for _ in range(TOP_K):
        best = jnp.max(logits, axis=-1, keepdims=True)
        arg = jnp.min(jnp.where(logits == best, lane, float(n_e)), axis=-1, keepdims=True)
        vals.append(best)
        idxs.append(arg)
        logits = jnp.where(lane == arg, -jnp.inf, logits)
    e = jnp.exp(jnp.concatenate(vals, axis=1) - vals[0])
    gate_ref[...] = e / jnp.sum(e, axis=-1, keepdims=True)
    idx_ref[...] = jnp.concatenate(idxs, axis=1).astype(I32)


def _xattn_router(h, xa_gain, wq, kx, vx, wo, moe_gain, w_router, b_router, *, tm=512):
    s, d = h.shape
    tm = min(tm, s)
    n_e = w_router.shape[1]
    w_hi = w_router.astype(BF16)
    w_router = jnp.stack([w_hi, (w_router - w_hi.astype(F32)).astype(BF16)])
    row = lambda width: pl.BlockSpec((tm, width), lambda i: (i, 0))
    return pl.pallas_call(
        _xattn_router_body,
        grid=(s // tm,),
        in_specs=[
            row(d), _const_spec((1, d)), _const_spec(wq.shape), _const_spec(kx.shape), _const_spec(vx.shape),
            _const_spec(wo.shape), _const_spec((1, d)), _const_spec(w_router.shape), _const_spec((1, n_e)),
        ],
        out_specs=[row(d), row(TOP_K), row(TOP_K)],
        out_shape=[
            jax.ShapeDtypeStruct((s, d), F32),
            jax.ShapeDtypeStruct((s, TOP_K), I32),
            jax.ShapeDtypeStruct((s, TOP_K), F32),
        ],
        compiler_params=_params("parallel"),
    )(h, xa_gain.reshape(1, d), wq, kx, vx, wo, moe_gain.reshape(1, d), w_router, b_router.reshape(1, n_e))


def _gather_body(tok_ref, off_ref, cnt_ref, dst_ref, h_hbm, gain_ref, o_ref, buf_ref, sem, *, rows, n_steps):
    del dst_ref
    g = pl.program_id(0)
    grains = range(rows // EXPERT_GRAIN)

    def row_copy(step, q, r):
        slot = step % 2
        return pltpu.make_async_copy(h_hbm.at[pl.ds(tok_ref[off_ref[step] + r], 1), :],
                                     buf_ref.at[slot, pl.ds(r, 1), :], sem.at[slot, q])

    for q in grains:
        @pl.when(jnp.logical_and(g < n_steps, cnt_ref[jnp.minimum(g, n_steps - 1)] > q * EXPERT_GRAIN))
        def _():
            for r in range(q * EXPERT_GRAIN, (q + 1) * EXPERT_GRAIN):
                row_copy(g, q, r).start()

    for q in grains:
        @pl.when(jnp.logical_and(g >= 1, cnt_ref[jnp.maximum(g - 1, 0)] > q * EXPERT_GRAIN))
        def _():
            def drain(r, carry):
                row_copy(g - 1, q, r).wait()
                return carry

            lax.fori_loop(q * EXPERT_GRAIN, (q + 1) * EXPERT_GRAIN, drain, 0, unroll=8)
            grain = pl.ds(q * EXPERT_GRAIN, EXPERT_GRAIN)
            o_ref[grain, :] = _rms(buf_ref[(g - 1) % 2, grain, :], gain_ref[...]).astype(o_ref.dtype)


def _gather_rows(h, gain, sorted_tok, sub_off, sub_cnt, sub_dst, n_rows):
    d = h.shape[1]
    rows = GATHER_ROWS
    n_steps = sub_dst.shape[0]
    return pl.pallas_call(
        functools.partial(_gather_body, rows=rows, n_steps=n_steps),
        grid_spec=pltpu.PrefetchScalarGridSpec(
            num_scalar_prefetch=4,
            grid=(n_steps + 1,),
            in_specs=[pl.BlockSpec(memory_space=pl.ANY), pl.BlockSpec((1, d), lambda g, *_: (0, 0))],
            out_specs=pl.BlockSpec((rows, d), lambda g, tok, off, cnt, dst: (dst[jnp.maximum(g - 1, 0)], 0)),
            scratch_shapes=[pltpu.VMEM((2, rows, d), F32), pltpu.SemaphoreType.DMA((2, rows // EXPERT_GRAIN))],
        ),
        out_shape=jax.ShapeDtypeStruct((n_rows, d), BF16),
        compiler_params=_params("arbitrary"),
    )(sorted_tok, sub_off, sub_cnt, sub_dst, h, gain.reshape(1, d))


def _expert_body(blk_e_ref, blk_src_ref, nsub_ref, x_ref, wg_ref, wl_ref, bg_ref, bl_ref, wd_ref, bd_ref,
                 o_ref, wgb_ref, wlb_ref, wdb_ref):
    del blk_e_ref, blk_src_ref
    b, j = pl.program_id(0), pl.program_id(1)
    nsub = nsub_ref[b]
    d = o_ref.shape[1]

    @pl.when(nsub > 0)
    def _():
        wgb_ref[...] = wg_ref[0].astype(BF16)
        wlb_ref[...] = wl_ref[0].astype(BF16)
        wdb_ref[...] = wd_ref[0].astype(BF16)

    @pl.when(jnp.logical_and(nsub > 0, j == 0))
    def _():
        o_ref[...] = jnp.broadcast_to(bd_ref[0], o_ref.shape)

    for m_sub in range(1, EXPERT_ROWS // EXPERT_GRAIN + 1):
        @pl.when(nsub == m_sub)
        def _():
            m = m_sub * EXPERT_GRAIN
            x = x_ref[0:m, :]
            hg = jnp.dot(x, wgb_ref[...], preferred_element_type=F32) + bg_ref[0]
            hl = jnp.dot(x, wlb_ref[...], preferred_element_type=F32) + bl_ref[0]
            g = jnp.minimum(hg, SWIGLU_LIMIT)
            l = jnp.clip(hl, -SWIGLU_LIMIT, SWIGLU_LIMIT)
            act = (g * jax.nn.sigmoid(SWIGLU_ALPHA * g) * (l + 1.0)).astype(BF16)
            for n in range(0, d, EXPERT_DOWN_CHUNK):
                cols = slice(n, min(n + EXPERT_DOWN_CHUNK, d))
                o_ref[0:m, cols] += jnp.dot(act, wdb_ref[:, cols], preferred_element_type=F32)


def _experts(xs, blk_expert, blk_src, blk_nsub, blocks_used, w_up, b_up, w_down, b_down):
    n_rows, d = xs.shape
    n_e, _, two_f = w_up.shape
    f = two_f // 2
    ft = min(EXPERT_FT, f)
    nj = f // ft
    nb = blocks_used

    def step(j, ns, b):
        return jnp.where(ns[b] > 0, j, nj - 1)

    return pl.pallas_call(
        _expert_body,
        grid_spec=pltpu.PrefetchScalarGridSpec(
            num_scalar_prefetch=3,
            grid=(nb, nj),
            in_specs=[
                pl.BlockSpec((EXPERT_ROWS, d), lambda b, j, be, bs, ns: (bs[b], 0)),
                pl.BlockSpec((1, d, ft), lambda b, j, be, bs, ns: (be[b], 0, step(j, ns, b))),
                pl.BlockSpec((1, d, ft), lambda b, j, be, bs, ns: (be[b], 0, nj + step(j, ns, b))),
                pl.BlockSpec((1, 1, ft), lambda b, j, be, bs, ns: (be[b], 0, step(j, ns, b))),
                pl.BlockSpec((1, 1, ft), lambda b, j, be, bs, ns: (be[b], 0, nj + step(j, ns, b))),
                pl.BlockSpec((1, ft, d), lambda b, j, be, bs, ns: (be[b], step(j, ns, b), 0)),
                pl.BlockSpec((1, 1, d), lambda b, j, be, bs, ns: (be[b], 0, 0)),
            ],
            out_specs=pl.BlockSpec((EXPERT_ROWS, d), lambda b, j, be, bs, ns: (bs[b], 0)),
            scratch_shapes=[pltpu.VMEM((d, ft), BF16), pltpu.VMEM((d, ft), BF16), pltpu.VMEM((ft, d), BF16)],
        ),
        out_shape=jax.ShapeDtypeStruct((n_rows, d), F32),
        compiler_params=_params("arbitrary", "arbitrary"),
    )(blk_expert, blk_src, blk_nsub, xs, w_up, w_up, b_up.reshape(n_e, 1, two_f), b_up.reshape(n_e, 1, two_f),
      w_down, b_down.reshape(n_e, 1, d))


def _combine_body(dest_ref, ys_hbm, gate_ref, h_ref, fg_ref, o_ref, buf_ref, sem, *, tt, n_tiles):
    i = pl.program_id(0)

    def row_copy(tile, t, k):
        slot = tile % 2
        return pltpu.make_async_copy(ys_hbm.at[pl.ds(dest_ref[(tile * tt + t) * TOP_K + k], 1), :],
                                     buf_ref.at[slot, k, pl.ds(t, 1), :], sem.at[slot])

    @pl.when(i < n_tiles)
    def _():
        for t in range(tt):
            for k in range(TOP_K):
                row_copy(i, t, k).start()

    @pl.when(i >= 1)
    def _():
        def drain(t, carry):
            for k in range(TOP_K):
                row_copy(i - 1, t, k).wait()
            return carry

        lax.fori_loop(0, tt, drain, 0, unroll=4)
        slot = (i - 1) % 2
        gate = gate_ref[...]
        y = h_ref[...]
        for k in range(TOP_K):
            y = y + gate[:, k:k + 1] * buf_ref[slot, k]
        o_ref[...] = _rms(y, fg_ref[...])


def _combine(ys, dest, gate, h, final_gain, *, tt=128):
    n, d = h.shape
    tt = min(tt, n)
    n_tiles = n // tt
    prev = lambda i, dst: (jnp.maximum(i - 1, 0), 0)
    return pl.pallas_call(
        functools.partial(_combine_body, tt=tt, n_tiles=n_tiles),
        grid_spec=pltpu.PrefetchScalarGridSpec(
            num_scalar_prefetch=1,
            grid=(n_tiles + 1,),
            in_specs=[
                pl.BlockSpec(memory_space=pl.ANY),
                pl.BlockSpec((tt, TOP_K), prev),
                pl.BlockSpec((tt, d), prev),
                pl.BlockSpec((1, d), lambda i, dst: (0, 0)),
            ],
            out_specs=pl.BlockSpec((tt, d), prev),
            scratch_shapes=[pltpu.VMEM((2, TOP_K, tt, d), F32), pltpu.SemaphoreType.DMA((2,))],
        ),
        out_shape=jax.ShapeDtypeStruct((n, d), F32),
        compiler_params=_params("arbitrary"),
    )(dest, ys, gate, h, final_gain.reshape(1, d))


def _routing_tables(top_idx, n_blocks, n_subs):
    assert EXPERT_ROWS % GATHER_ROWS == 0 and GATHER_ROWS % EXPERT_GRAIN == 0
    flat_e = top_idx.reshape(-1)
    nk = flat_e.shape[0]
    ids = jnp.arange(nk, dtype=I32)
    onehot = (flat_e[:, None] == jnp.arange(N_EXPERTS, dtype=I32)[None, :]).astype(I32)
    seen = jnp.cumsum(onehot, axis=0)
    rank = jnp.take_along_axis(seen, flat_e[:, None], axis=1)[:, 0] - 1
    counts = seen[-1]
    first = jnp.cumsum(counts) - counts
    subs_per_block = EXPERT_ROWS // GATHER_ROWS

    blocks_per_e = (counts + EXPERT_ROWS - 1) // EXPERT_ROWS
    blk_end = jnp.cumsum(blocks_per_e)
    blk_start = blk_end - blocks_per_e
    dest = blk_start[flat_e] * EXPERT_ROWS + rank

    sorted_tok = (jnp.sort(flat_e * nk + ids) % nk) // TOP_K
    sorted_tok = jnp.concatenate([sorted_tok, jnp.arange(GATHER_ROWS, dtype=I32) % (nk // TOP_K)])
    subs_per_e = (counts + GATHER_ROWS - 1) // GATHER_ROWS
    sub_end = jnp.cumsum(subs_per_e)
    sub_start = sub_end - subs_per_e
    g = jnp.arange(n_subs, dtype=I32)
    g_src = jnp.minimum(g, sub_end[-1] - 1)
    g_expert = jnp.minimum(jnp.sum(sub_end[None, :] <= g_src[:, None], axis=1), N_EXPERTS - 1)
    g_sub = g_src - sub_start[g_expert]
    sub_dst = blk_start[g_expert] * subs_per_block + g_sub
    sub_off = first[g_expert] + g_sub * GATHER_ROWS
    sub_cnt = jnp.where(g < sub_end[-1], jnp.clip(counts[g_expert] - g_sub * GATHER_ROWS, 0, GATHER_ROWS), 0)

    b = jnp.arange(n_blocks, dtype=I32)
    used = blk_end[-1]
    src = jnp.minimum(b, used - 1)
    blk_expert = jnp.minimum(jnp.sum(blk_end[None, :] <= src[:, None], axis=1), N_EXPERTS - 1).astype(I32)
    rows_here = jnp.clip(counts[blk_expert] - (src - blk_start[blk_expert]) * EXPERT_ROWS, 0, EXPERT_ROWS)
    grains = jnp.where(b < used, (rows_here + EXPERT_GRAIN - 1) // EXPERT_GRAIN, 0)
    i32 = lambda a: a.astype(I32)
    return (i32(dest), i32(sorted_tok), i32(sub_off), i32(sub_cnt), i32(sub_dst),
            blk_expert, i32(src), i32(grains), i32(used))


def kernel(x, mem, norm_mix, w_in, pool_lin, pool_scale, w_branch_a, w_branch_b, w_gate, b_gate, w_mix_out,
           xa_norm, mem_norm, xa_wq, xa_wkv, xa_wo, moe_norm, w_router, b_router, w_up, b_up, w_down, b_down,
           final_norm):
    batch, s, d = x.shape
    assert batch == 1 and norm_mix.shape[0] == 1
    h = x.reshape(s, d)
    l = 0
    bf = lambda w: w.astype(BF16)

    qkv_w = 3 * SB_WIDTH
    q_scale = jnp.concatenate([jnp.full((SB_WIDTH,), SB_HEAD_DIM ** -0.5, F32), jnp.ones((2 * SB_WIDTH,), F32)])
    qkv = _linear(h, bf(w_in[l][:, :qkv_w]), gain=norm_mix[l], col_scale=q_scale, out_dtype=BF16)
    p = _linear(h, bf(w_in[l][:, qkv_w:]), gain=norm_mix[l], out_dtype=F32)
    gates = _linear(h, bf(w_gate[l]), gain=norm_mix[l], bias=b_gate[l], act="sigmoid", out_dtype=BF16)
    o_a = _sb_attention(qkv, heads=SB_HEADS)
    o_b = _pool(p, bf(pool_lin[l]), pool_scale[l])
    h = _merge(o_a, o_b, gates, h, bf(w_branch_a[l]), bf(w_branch_b[l]), bf(w_mix_out[l]))

    kv = _linear(mem.reshape(mem.shape[1], d), bf(xa_wkv[l]), gain=mem_norm[l], out_dtype=BF16)
    h, top_idx, gate = _xattn_router(h, xa_norm[l], bf(xa_wq[l]), kv[:, :XA_WIDTH], kv[:, XA_WIDTH:],
                                     bf(xa_wo[l]), moe_norm[l], w_router[l], b_router[l])

    n_blocks = (s * TOP_K) // EXPERT_ROWS + N_EXPERTS
    n_subs = (s * TOP_K) // GATHER_ROWS + N_EXPERTS
    (dest, sorted_tok, sub_off, sub_cnt, sub_dst,
     blk_expert, blk_src, blk_grains, blocks_used) = _routing_tables(top_idx, n_blocks, n_subs)
    xs = _gather_rows(h, moe_norm[l], sorted_tok, sub_off, sub_cnt, sub_dst, n_blocks * EXPERT_ROWS)
    ys = _experts(xs, blk_expert, blk_src, blk_grains, blocks_used, w_up[l], b_up[l], w_down[l], b_down[l])
    out = _combine(ys, dest, gate, h, final_norm)
    return out.reshape(batch, s, d)
```

```python
import functools

import jax
import jax.numpy as jnp
from jax import lax
from jax.experimental import pallas as pl
from jax.experimental.pallas import tpu as pltpu

F32 = jnp.float32
BF16 = jnp.bfloat16
U32 = jnp.uint32
I32 = jnp.int32

RMS_EPS = 1e-5

SB_HEADS = 8
SB_HEAD_DIM = 128
SB_WIDTH = SB_HEADS * SB_HEAD_DIM
POOL_WINDOWS = (2, 4, 8, 16)
POOL_GROUP_DIM = 256
POOL_WIDTH = len(POOL_WINDOWS) * POOL_GROUP_DIM
POOL_HALO = 16
XA_HEADS = 4
XA_HEAD_DIM = 128
XA_WIDTH = XA_HEADS * XA_HEAD_DIM
N_EXPERTS = 32
TOP_K = 4
SWIGLU_ALPHA = 1.702
SWIGLU_LIMIT = 7.0

VMEM_LIMIT_BYTES = 56 * 1024 * 1024

SB_DEAD_LOG = 110.0

EXPERT_ROWS = 1280
EXPERT_GRAIN = 128
EXPERT_FT = 256
EXPERT_DOWN_CHUNK = 512


def _params(*sem):
    return pltpu.CompilerParams(dimension_semantics=sem, vmem_limit_bytes=VMEM_LIMIT_BYTES)


def _rms(x, gain):
    return x * lax.rsqrt(jnp.mean(x * x, axis=-1, keepdims=True) + RMS_EPS) * gain


def _linear_body(*refs, rms, has_scale, has_bias, act):
    refs = list(refs)
    a_ref = refs.pop(0)
    g_ref = refs.pop(0) if rms else None
    w_ref = refs.pop(0)
    s_ref = refs.pop(0) if has_scale else None
    b_ref = refs.pop(0) if has_bias else None
    o_ref, abf_ref = refs

    @pl.when(pl.program_id(1) == 0)
    def _():
        a = a_ref[...].astype(F32)
        if rms:
            a = _rms(a, g_ref[...])
        abf_ref[...] = a.astype(BF16)

    acc = jnp.dot(abf_ref[...], w_ref[...], preferred_element_type=F32)
    if has_scale:
        acc = acc * s_ref[...]
    if has_bias:
        acc = acc + b_ref[...]
    if act == "sigmoid":
        acc = jax.nn.sigmoid(acc)
    o_ref[...] = acc.astype(o_ref.dtype)


def _linear(a, w, *, gain=None, col_scale=None, bias=None, act=None, out_dtype=F32, tm=1024, tn=1024):
    m, k = a.shape
    n = w.shape[1]
    tm, tn = min(tm, m), min(tn, n)
    assert m % tm == 0 and n % tn == 0
    args, specs = [a], [pl.BlockSpec((tm, k), lambda i, j: (i, 0))]
    if gain is not None:
        args.append(gain.reshape(1, k).astype(F32))
        specs.append(pl.BlockSpec((1, k), lambda i, j: (0, 0)))
    args.append(w)
    specs.append(pl.BlockSpec((k, tn), lambda i, j: (0, j)))
    for vec in (col_scale, bias):
        if vec is not None:
            args.append(vec.reshape(1, n).astype(F32))
            specs.append(pl.BlockSpec((1, tn), lambda i, j: (0, j)))
    body = functools.partial(_linear_body, rms=gain is not None, has_scale=col_scale is not None,
                             has_bias=bias is not None, act=act)
    return pl.pallas_call(
        body,
        grid=(m // tm, n // tn),
        in_specs=specs,
        out_specs=pl.BlockSpec((tm, tn), lambda i, j: (i, j)),
        out_shape=jax.ShapeDtypeStruct((m, n), out_dtype),
        scratch_shapes=[pltpu.VMEM((tm, k), BF16)],
        compiler_params=_params("parallel", "arbitrary"),
    )(*args)


def _sb_attn_body(q_ref, k_ref, v_ref, o_ref, acc_ref, run_ref, *, t, group):
    i = pl.program_id(1)
    dh = SB_HEAD_DIM
    row = lax.broadcasted_iota(I32, (t, t), 0)
    col = lax.broadcasted_iota(I32, (t, t), 1)
    causal = col < row
    later_keys = (row > col).astype(BF16)
    later_keys = jnp.concatenate([later_keys, later_keys], axis=0)
    heads = range(group)
    head_cols = [slice(h * dh, (h + 1) * dh) for h in heads]

    def tile(kb, diagonal):
        start = pl.multiple_of(kb * t, t)
        z = [lax.dot_general(q_ref[:, c], k_ref[pl.ds(start, t), c], (((1,), (1,)), ((), ())),
                             preferred_element_type=F32) for c in head_cols]
        softplus = [jnp.maximum(x, 0.0) + jnp.log(1.0 + jnp.exp(-jnp.abs(x))) for x in z]
        neg_log_keep = [jnp.where(causal, x, 0.0) for x in softplus] if diagonal else softplus
        later = []
        for x in neg_log_keep:
            hi = x.astype(BF16)
            lo = (x - hi.astype(F32)).astype(BF16)
            later.append(jnp.dot(jnp.concatenate([hi, lo], axis=1), later_keys, preferred_element_type=F32))
        run_min = None
        for h in heads:
            if diagonal:
                w = jnp.where(causal, jnp.exp(z[h] - softplus[h] - later[h]), 0.0)
                run = jnp.sum(neg_log_keep[h], axis=1, keepdims=True)
            else:
                run = run_ref[h]
                w = jnp.exp(z[h] - softplus[h] - later[h] - run)
                run = run + jnp.sum(neg_log_keep[h], axis=1, keepdims=True)
            pv = jnp.dot(w.astype(BF16), v_ref[pl.ds(start, t), head_cols[h]], preferred_element_type=F32)
            if diagonal:
                acc_ref[h] = pv
            else:
                acc_ref[h] += pv
            run_ref[h] = run
            head_min = jnp.min(run)
            run_min = head_min if run_min is None else jnp.minimum(run_min, head_min)
        return run_min

    def cond(carry):
        kb, run_min = carry
        return jnp.logical_and(kb >= 0, run_min < SB_DEAD_LOG)

    def step(carry):
        kb, _ = carry
        return kb - 1, tile(kb, False)

    lax.while_loop(cond, step, (i - 1, tile(i, True)))
    for h in range(group):
        o_ref[:, h * dh:(h + 1) * dh] = acc_ref[h].astype(o_ref.dtype)


def _sb_attention(qkv, *, heads, t=256, group=4):
    s = qkv.shape[0]
    dh = SB_HEAD_DIM
    t, group = min(t, s), min(group, heads)
    assert s % t == 0 and heads % group == 0
    n_groups = heads // group
    width = group * dh
    return pl.pallas_call(
        functools.partial(_sb_attn_body, t=t, group=group),
        grid=(n_groups, s // t),
        in_specs=[
            pl.BlockSpec((t, width), lambda g, i: (i, g)),
            pl.BlockSpec((s, width), lambda g, i: (0, n_groups + g)),
            pl.BlockSpec((s, width), lambda g, i: (0, 2 * n_groups + g)),
        ],
        out_specs=pl.BlockSpec((t, width), lambda g, i: (i, g)),
        out_shape=jax.ShapeDtypeStruct((s, heads * dh), BF16),
        scratch_shapes=[pltpu.VMEM((group, t, dh), F32), pltpu.VMEM((group, t, 1), F32)],
        compiler_params=_params("parallel", "arbitrary"),
    )(qkv, qkv, qkv)


def _pool_body(p_ref, halo_ref, lin_ref, scale_ref, o_ref, *, ts):
    i = pl.program_id(0)
    cur = p_ref[...]
    halo = jnp.where(i > 0, halo_ref[...], 0.0)
    ext = jnp.concatenate([halo, cur], axis=0)
    pos = i * ts + lax.broadcasted_iota(I32, (ts, 1), 0)
    c = POOL_GROUP_DIM
    for g, window in enumerate(POOL_WINDOWS):
        total = ext[:, g * c:(g + 1) * c]
        span = 1
        while span < window:
            total = total + pltpu.roll(total, span, 0)
            span *= 2
        total = total[POOL_HALO:]
        count = jnp.minimum(pos + 1, window).astype(F32)
        pooled = total / count - cur[:, g * c:(g + 1) * c]
        mixed = jnp.dot(pooled.astype(BF16), lin_ref[g], preferred_element_type=F32)
        o_ref[:, g * c:(g + 1) * c] = (mixed * scale_ref[:, g * c:(g + 1) * c]).astype(o_ref.dtype)


def _pool(p, pool_lin, pool_scale, *, ts=512):
    s, width = p.shape
    ts = min(ts, s)
    assert s % ts == 0 and ts % POOL_HALO == 0
    per = ts // POOL_HALO
    groups = len(POOL_WINDOWS)
    return pl.pallas_call(
        functools.partial(_pool_body, ts=ts),
        grid=(s // ts,),
        in_specs=[
            pl.BlockSpec((ts, width), lambda i: (i, 0)),
            pl.BlockSpec((POOL_HALO, width), lambda i: (jnp.maximum(i * per - 1, 0), 0)),
            pl.BlockSpec((groups, POOL_GROUP_DIM, POOL_GROUP_DIM), lambda i: (0, 0, 0)),
            pl.BlockSpec((1, width), lambda i: (0, 0)),
        ],
        out_specs=pl.BlockSpec((ts, width), lambda i: (i, 0)),
        out_shape=jax.ShapeDtypeStruct((s, width), BF16),
        compiler_params=_params("parallel"),
    )(p, p, pool_lin, pool_scale.reshape(1, width).astype(F32))


def _merge_body(oa_ref, ob_ref, ga_ref, gb_ref, h_ref, wa_ref, wb_ref, wm_ref, o_ref):
    ya = jnp.dot(oa_ref[...], wa_ref[...], preferred_element_type=F32)
    yb = jnp.dot(ob_ref[...], wb_ref[...], preferred_element_type=F32)
    merged = ga_ref[...].astype(F32) * ya + gb_ref[...].astype(F32) * yb
    o_ref[...] = h_ref[...] + jnp.dot(merged.astype(BF16), wm_ref[...], preferred_element_type=F32)


def _const_spec(shape):
    return pl.BlockSpec(shape, lambda i: (0,) * len(shape), pipeline_mode=pl.Buffered(1))


def _merge(o_a, o_b, gates, h, w_a, w_b, w_mix, *, tm=256):
    s, d = h.shape
    tm = min(tm, s)
    return pl.pallas_call(
        _merge_body,
        grid=(s // tm,),
        in_specs=[
            pl.BlockSpec((tm, o_a.shape[1]), lambda i: (i, 0)),
            pl.BlockSpec((tm, o_b.shape[1]), lambda i: (i, 0)),
            pl.BlockSpec((tm, d), lambda i: (i, 0)),
            pl.BlockSpec((tm, d), lambda i: (i, 1)),
            pl.BlockSpec((tm, d), lambda i: (i, 0)),
            _const_spec(w_a.shape), _const_spec(w_b.shape), _const_spec(w_mix.shape),
        ],
        out_specs=pl.BlockSpec((tm, d), lambda i: (i, 0)),
        out_shape=jax.ShapeDtypeStruct((s, d), F32),
        compiler_params=_params("parallel"),
    )(o_a, o_b, gates, gates, h, w_a, w_b, w_mix)


def _xattn_router_body(h_ref, xg_ref, wq_ref, k_ref, v_ref, wo_ref, mg_ref, wr_ref, br_ref,
                       h2_ref, idx_ref, gate_ref):
    h = h_ref[...]
    u = _rms(h, xg_ref[...]).astype(BF16)
    q = (jnp.dot(u, wq_ref[...], preferred_element_type=F32) * (XA_HEAD_DIM ** -0.5)).astype(BF16)
    heads = []
    for hd in range(XA_HEADS):
        sl = slice(hd * XA_HEAD_DIM, (hd + 1) * XA_HEAD_DIM)
        s = lax.dot_general(q[:, sl], k_ref[:, sl], (((1,), (1,)), ((), ())), preferred_element_type=F32)
        p = jnp.exp(s - jnp.max(s, axis=-1, keepdims=True))
        o = jnp.dot(p.astype(BF16), v_ref[:, sl], preferred_element_type=F32)
        heads.append(o / jnp.sum(p, axis=-1, keepdims=True))
    o = jnp.concatenate(heads, axis=1).astype(BF16)
    h2 = h + jnp.dot(o, wo_ref[...], preferred_element_type=F32)
    h2_ref[...] = h2

    t = _rms(h2, mg_ref[...])
    t_hi = t.astype(BF16)
    t_lo = (t - t_hi.astype(F32)).astype(BF16)
    logits = (jnp.dot(t_hi, wr_ref[0], preferred_element_type=F32)
              + jnp.dot(t_hi, wr_ref[1], preferred_element_type=F32)
              + jnp.dot(t_lo, wr_ref[0], preferred_element_type=F32)) + br_ref[...]
    n_e = logits.shape[1]
    lane = lax.broadcasted_iota(I32, logits.shape, 1).astype(F32)
    vals, idxs = [], []
    for _ in range(TOP_K):
        best = jnp.max(logits, axis=-1, keepdims=True)
        arg = jnp.min(jnp.where(logits == best, lane, float(n_e)), axis=-1, keepdims=True)
        vals.append(best)
        idxs.append(arg)
        logits = jnp.where(lane == arg, -jnp.inf, logits)
    e = jnp.exp(jnp.concatenate(vals, axis=1) - vals[0])
    gate_ref[...] = e / jnp.sum(e, axis=-1, keepdims=True)
    idx_ref[...] = jnp.concatenate(idxs, axis=1).astype(I32)


def _xattn_router(h, xa_gain, wq, kx, vx, wo, moe_gain, w_router, b_router, *, tm=512):
    s, d = h.shape
    tm = min(tm, s)
    n_e = w_router.shape[1]
    w_hi = w_router.astype(BF16)
    w_router = jnp.stack([w_hi, (w_router - w_hi.astype(F32)).astype(BF16)])
    row = lambda width: pl.BlockSpec((tm, width), lambda i: (i, 0))
    return pl.pallas_call(
        _xattn_router_body,
        grid=(s // tm,),
        in_specs=[
            row(d), _const_spec((1, d)), _const_spec(wq.shape), _const_spec(kx.shape), _const_spec(vx.shape),
            _const_spec(wo.shape), _const_spec((1, d)), _const_spec(w_router.shape), _const_spec((1, n_e)),
        ],
        out_specs=[row(d), row(TOP_K), row(TOP_K)],
        out_shape=[
            jax.ShapeDtypeStruct((s, d), F32),
            jax.ShapeDtypeStruct((s, TOP_K), I32),
            jax.ShapeDtypeStruct((s, TOP_K), F32),
        ],
        compiler_params=_params("parallel"),
    )(h, xa_gain.reshape(1, d), wq, kx, vx, wo, moe_gain.reshape(1, d), w_router, b_router.reshape(1, n_e))


def _expert_body(blk_e_ref, blk_src_ref, grains_ref, off_ref, tok_ref,
                 h_hbm, gain_ref, wup_hbm, wdn_hbm, bup_ref, bdn_ref, o_ref,
                 xst_ref, xb_ref, wg_st, wl_st, wd_st, wgb_ref, wlb_ref, wdb_ref, wsem, xsem, *, nj, ft, n_blocks):
    del blk_src_ref
    b = pl.program_id(0)
    grains = grains_ref[b]
    nxt = jnp.minimum(b + 1, n_blocks - 1)
    has_next = jnp.logical_and(b + 1 < n_blocks, grains_ref[nxt] > 0)
    d = o_ref.shape[1]
    n_grains = EXPERT_ROWS // EXPERT_GRAIN

    def weight_copies(blk, j, slot):
        e = blk_e_ref[blk]
        c = pl.multiple_of(j * ft, ft)
        return (
            pltpu.make_async_copy(wup_hbm.at[e, :, pl.ds(c, ft)], wg_st.at[slot], wsem.at[slot, 0]),
            pltpu.make_async_copy(wup_hbm.at[e, :, pl.ds(nj * ft + c, ft)], wl_st.at[slot], wsem.at[slot, 1]),
            pltpu.make_async_copy(wdn_hbm.at[e, pl.ds(c, ft), :], wd_st.at[slot], wsem.at[slot, 2]),
        )

    def row_copy(blk, q, r):
        return pltpu.make_async_copy(h_hbm.at[pl.ds(tok_ref[off_ref[blk] + r], 1), :],
                                     xst_ref.at[pl.ds(r, 1), :], xsem.at[q])

    def start_rows(blk, q):
        for r in range(q * EXPERT_GRAIN, (q + 1) * EXPERT_GRAIN):
            row_copy(blk, q, r).start()

    @pl.when(b == 0)
    def _():
        for copy in weight_copies(0, 0, 0):
            copy.start()
        for q in range(n_grains):
            @pl.when(q < grains)
            def _():
                start_rows(0, q)

    @pl.when(grains > 0)
    def _():
        for q in range(n_grains):
            @pl.when(q < grains)
            def _():
                def drain(r, carry):
                    row_copy(b, q, r).wait()
                    return carry

                lax.fori_loop(q * EXPERT_GRAIN, (q + 1) * EXPERT_GRAIN, drain, 0, unroll=8)
                grain = pl.ds(q * EXPERT_GRAIN, EXPERT_GRAIN)
                xb_ref[grain, :] = _rms(xst_ref[grain, :], gain_ref[...]).astype(BF16)

        o_ref[...] = jnp.broadcast_to(bdn_ref[0], o_ref.shape)

        def hidden_tile(j, carry):
            slot = j % 2
            for copy in weight_copies(b, j, slot):
                copy.wait()

            @pl.when(j + 1 < nj)
            def _():
                for copy in weight_copies(b, j + 1, 1 - slot):
                    copy.start()

            @pl.when(jnp.logical_and(j + 1 == nj, has_next))
            def _():
                for copy in weight_copies(nxt, 0, 1 - slot):
                    copy.start()

            wgb_ref[...] = wg_st[slot].astype(BF16)
            wlb_ref[...] = wl_st[slot].astype(BF16)
            wdb_ref[...] = wd_st[slot].astype(BF16)

            for q in range(n_grains):
                @pl.when(jnp.logical_and(jnp.logical_and(has_next, q < grains_ref[nxt]), j == q % nj))
                def _():
                    start_rows(nxt, q)

            bias_g = bup_ref[0, pl.ds(j, 1), :]
            bias_l = bup_ref[0, pl.ds(nj + j, 1), :]
            for m_grains in range(1, n_grains + 1):
                @pl.when(grains == m_grains)
                def _():
                    m = m_grains * EXPERT_GRAIN
                    x = xb_ref[0:m, :]
                    hg = jnp.dot(x, wgb_ref[...], preferred_element_type=F32) + bias_g
                    hl = jnp.dot(x, wlb_ref[...], preferred_element_type=F32) + bias_l
                    g = jnp.minimum(hg, SWIGLU_LIMIT)
                    l = jnp.clip(hl, -SWIGLU_LIMIT, SWIGLU_LIMIT)
                    act = (g * jax.nn.sigmoid(SWIGLU_ALPHA * g) * (l + 1.0)).astype(BF16)
                    for n in range(0, d, EXPERT_DOWN_CHUNK):
                        cols = slice(n, min(n + EXPERT_DOWN_CHUNK, d))
                        o_ref[0:m, cols] += jnp.dot(act, wdb_ref[:, cols], preferred_element_type=F32)
            return carry

        lax.fori_loop(0, nj, hidden_tile, 0)


def _experts(h, gain, sorted_tok, blk_expert, blk_src, blk_grains, blk_off, w_up, b_up, w_down, b_down):
    d = h.shape[1]
    n_e, _, two_f = w_up.shape
    f = two_f // 2
    ft = min(EXPERT_FT, f)
    nj = f // ft
    assert nj % 2 == 0
    nb = blk_expert.shape[0]
    n_grains = EXPERT_ROWS // EXPERT_GRAIN
    by_expert = lambda b, be, bs, gr, off, tok: (be[b], 0, 0)
    return pl.pallas_call(
        functools.partial(_expert_body, nj=nj, ft=ft, n_blocks=nb),
        grid_spec=pltpu.PrefetchScalarGridSpec(
            num_scalar_prefetch=5,
            grid=(nb,),
            in_specs=[
                pl.BlockSpec(memory_space=pl.ANY),
                pl.BlockSpec((1, d), lambda b, *_: (0, 0)),
                pl.BlockSpec(memory_space=pl.ANY),
                pl.BlockSpec(memory_space=pl.ANY),
                pl.BlockSpec((1, 2 * nj, ft), by_expert),
                pl.BlockSpec((1, 1, d), by_expert),
            ],
            out_specs=pl.BlockSpec((EXPERT_ROWS, d), lambda b, be, bs, gr, off, tok: (bs[b], 0)),
            scratch_shapes=[
                pltpu.VMEM((EXPERT_ROWS, d), F32), pltpu.VMEM((EXPERT_ROWS, d), BF16),
                pltpu.VMEM((2, d, ft), F32), pltpu.VMEM((2, d, ft), F32), pltpu.VMEM((2, ft, d), F32),
                pltpu.VMEM((d, ft), BF16), pltpu.VMEM((d, ft), BF16), pltpu.VMEM((ft, d), BF16),
                pltpu.SemaphoreType.DMA((2, 3)), pltpu.SemaphoreType.DMA((n_grains,)),
            ],
        ),
        out_shape=jax.ShapeDtypeStruct((nb * EXPERT_ROWS, d), F32),
        compiler_params=_params("arbitrary"),
    )(blk_expert, blk_src, blk_grains, blk_off, sorted_tok,
      h, gain.reshape(1, d), w_up, w_down, b_up.reshape(n_e, 2 * nj, ft), b_down.reshape(n_e, 1, d))


def _combine_body(dest_ref, ys_hbm, gate_ref, h_ref, fg_ref, o_ref, buf_ref, sem, *, tt, n_tiles):
    i = pl.program_id(0)

    def row_copy(tile, t, k):
        slot = tile % 2
        return pltpu.make_async_copy(ys_hbm.at[pl.ds(dest_ref[(tile * tt + t) * TOP_K + k], 1), :],
                                     buf_ref.at[slot, k, pl.ds(t, 1), :], sem.at[slot])

    @pl.when(i < n_tiles)
    def _():
        for t in range(tt):
            for k in range(TOP_K):
                row_copy(i, t, k).start()

    @pl.when(i >= 1)
    def _():
        def drain(t, carry):
            for k in range(TOP_K):
                row_copy(i - 1, t, k).wait()
            return carry

        lax.fori_loop(0, tt, drain, 0, unroll=4)
        slot = (i - 1) % 2
        gate = gate_ref[...]
        y = h_ref[...]
        for k in range(TOP_K):
            y = y + gate[:, k:k + 1] * buf_ref[slot, k]
        o_ref[...] = _rms(y, fg_ref[...])


def _combine(ys, dest, gate, h, final_gain, *, tt=128):
    n, d = h.shape
    tt = min(tt, n)
    n_tiles = n // tt
    prev = lambda i, dst: (jnp.maximum(i - 1, 0), 0)
    return pl.pallas_call(
        functools.partial(_combine_body, tt=tt, n_tiles=n_tiles),
        grid_spec=pltpu.PrefetchScalarGridSpec(
            num_scalar_prefetch=1,
            grid=(n_tiles + 1,),
            in_specs=[
                pl.BlockSpec(memory_space=pl.ANY),
                pl.BlockSpec((tt, TOP_K), prev),
                pl.BlockSpec((tt, d), prev),
                pl.BlockSpec((1, d), lambda i, dst: (0, 0)),
            ],
            out_specs=pl.BlockSpec((tt, d), prev),
            scratch_shapes=[pltpu.VMEM((2, TOP_K, tt, d), F32), pltpu.SemaphoreType.DMA((2,))],
        ),
        out_shape=jax.ShapeDtypeStruct((n, d), F32),
        compiler_params=_params("arbitrary"),
    )(dest, ys, gate, h, final_gain.reshape(1, d))


def _routing_tables(top_idx, n_blocks):
    flat_e = top_idx.reshape(-1)
    nk = flat_e.shape[0]
    ids = jnp.arange(nk, dtype=I32)
    experts = jnp.arange(N_EXPERTS, dtype=I32)
    counts = jnp.sum((flat_e[:, None] == experts[None, :]).astype(I32), axis=0)
    first = jnp.cumsum(counts) - counts
    blocks_per_e = (counts + EXPERT_ROWS - 1) // EXPERT_ROWS
    blk_end = jnp.cumsum(blocks_per_e)
    blk_start = blk_end - blocks_per_e

    order = jnp.sort(flat_e * nk + ids)
    sorted_id = order % nk
    shift = blk_start * EXPERT_ROWS - first
    sorted_dest = ids + jnp.take(shift, order // nk)
    _, dest = lax.sort_key_val(sorted_id, sorted_dest)
    sorted_tok = jnp.concatenate([sorted_id // TOP_K, jnp.arange(EXPERT_GRAIN, dtype=I32) % (nk // TOP_K)])

    b = jnp.arange(n_blocks, dtype=I32)
    used = blk_end[-1]
    src = jnp.minimum(b, used - 1)
    blk_expert = jnp.minimum(jnp.sum((blk_end[None, :] <= src[:, None]).astype(I32), axis=1), N_EXPERTS - 1)
    row0 = (src - blk_start[blk_expert]) * EXPERT_ROWS
    rows_here = jnp.clip(counts[blk_expert] - row0, 0, EXPERT_ROWS)
    grains = jnp.where(b < used, (rows_here + EXPERT_GRAIN - 1) // EXPERT_GRAIN, 0)
    blk_off = first[blk_expert] + row0
    i32 = lambda a: a.astype(I32)
    return i32(dest), i32(sorted_tok), i32(blk_expert), i32(src), i32(grains), i32(blk_off)


def kernel(x, mem, norm_mix, w_in, pool_lin, pool_scale, w_branch_a, w_branch_b, w_gate, b_gate, w_mix_out,
           xa_norm, mem_norm, xa_wq, xa_wkv, xa_wo, moe_norm, w_router, b_router, w_up, b_up, w_down, b_down,
           final_norm):
    batch, s, d = x.shape
    assert batch == 1 and norm_mix.shape[0] == 1
    h = x.reshape(s, d)
    l = 0
    bf = lambda w: w.astype(BF16)

    qkv_w = 3 * SB_WIDTH
    q_scale = jnp.concatenate([jnp.full((SB_WIDTH,), SB_HEAD_DIM ** -0.5, F32), jnp.ones((2 * SB_WIDTH,), F32)])
    qkv = _linear(h, bf(w_in[l][:, :qkv_w]), gain=norm_mix[l], col_scale=q_scale, out_dtype=BF16)
    p = _linear(h, bf(w_in[l][:, qkv_w:]), gain=norm_mix[l], out_dtype=F32)
    gates = _linear(h, bf(w_gate[l]), gain=norm_mix[l], bias=b_gate[l], act="sigmoid", out_dtype=BF16)
    o_a = _sb_attention(qkv, heads=SB_HEADS)
    o_b = _pool(p, bf(pool_lin[l]), pool_scale[l])
    h = _merge(o_a, o_b, gates, h, bf(w_branch_a[l]), bf(w_branch_b[l]), bf(w_mix_out[l]))

    kv = _linear(mem.reshape(mem.shape[1], d), bf(xa_wkv[l]), gain=mem_norm[l], out_dtype=BF16)
    h, top_idx, gate = _xattn_router(h, xa_norm[l], bf(xa_wq[l]), kv[:, :XA_WIDTH], kv[:, XA_WIDTH:],
                                     bf(xa_wo[l]), moe_norm[l], w_router[l], b_router[l])

    n_blocks = (s * TOP_K) // EXPERT_ROWS + N_EXPERTS
    dest, sorted_tok, blk_expert, blk_src, blk_grains, blk_off = _routing_tables(top_idx, n_blocks)
    ys = _experts(h, moe_norm[l], sorted_tok, blk_expert, blk_src, blk_grains, blk_off,
                  w_up[l], b_up[l], w_down[l], b_down[l])
    out = _combine(ys, dest, gate, h, final_norm)
    return out.reshape(batch, s, d)
```

```python
import functools

import jax
import jax.numpy as jnp
from jax import lax
from jax.experimental import pallas as pl
from jax.experimental.pallas import tpu as pltpu

F32 = jnp.float32
BF16 = jnp.bfloat16
U32 = jnp.uint32
I32 = jnp.int32

RMS_EPS = 1e-5

SB_HEADS = 8
SB_HEAD_DIM = 128
SB_WIDTH = SB_HEADS * SB_HEAD_DIM
POOL_WINDOWS = (2, 4, 8, 16)
POOL_GROUP_DIM = 256
POOL_WIDTH = len(POOL_WINDOWS) * POOL_GROUP_DIM
POOL_HALO = 16
XA_HEADS = 4
XA_HEAD_DIM = 128
XA_WIDTH = XA_HEADS * XA_HEAD_DIM
N_EXPERTS = 32
TOP_K = 4
SWIGLU_ALPHA = 1.702
SWIGLU_LIMIT = 7.0

VMEM_LIMIT_BYTES = 56 * 1024 * 1024
EXPERT_VMEM_LIMIT_BYTES = 60 * 1024 * 1024

SB_DEAD_LOG = 110.0

EXPERT_ROWS = 1280
EXPERT_GRAIN = 128
EXPERT_FT = 256
EXPERT_DOWN_CHUNK = 512


def _params(*sem, vmem_limit_bytes=VMEM_LIMIT_BYTES):
    return pltpu.CompilerParams(dimension_semantics=sem, vmem_limit_bytes=vmem_limit_bytes)


def _rms(x, gain):
    return x * lax.rsqrt(jnp.mean(x * x, axis=-1, keepdims=True) + RMS_EPS) * gain


def _linear_body(*refs, rms, has_scale, has_bias, act):
    refs = list(refs)
    a_ref = refs.pop(0)
    g_ref = refs.pop(0) if rms else None
    w_ref = refs.pop(0)
    s_ref = refs.pop(0) if has_scale else None
    b_ref = refs.pop(0) if has_bias else None
    o_ref, abf_ref = refs

    @pl.when(pl.program_id(1) == 0)
    def _():
        a = a_ref[...].astype(F32)
        if rms:
            a = _rms(a, g_ref[...])
        abf_ref[...] = a.astype(BF16)

    acc = jnp.dot(abf_ref[...], w_ref[...], preferred_element_type=F32)
    if has_scale:
        acc = acc * s_ref[...]
    if has_bias:
        acc = acc + b_ref[...]
    if act == "sigmoid":
        acc = jax.nn.sigmoid(acc)
    o_ref[...] = acc.astype(o_ref.dtype)


def _linear(a, w, *, gain=None, col_scale=None, bias=None, act=None, out_dtype=F32, tm=1024, tn=1024):
    m, k = a.shape
    n = w.shape[1]
    tm, tn = min(tm, m), min(tn, n)
    assert m % tm == 0 and n % tn == 0
    args, specs = [a], [pl.BlockSpec((tm, k), lambda i, j: (i, 0))]
    if gain is not None:
        args.append(gain.reshape(1, k).astype(F32))
        specs.append(pl.BlockSpec((1, k), lambda i, j: (0, 0)))
    args.append(w)
    specs.append(pl.BlockSpec((k, tn), lambda i, j: (0, j)))
    for vec in (col_scale, bias):
        if vec is not None:
            args.append(vec.reshape(1, n).astype(F32))
            specs.append(pl.BlockSpec((1, tn), lambda i, j: (0, j)))
    body = functools.partial(_linear_body, rms=gain is not None, has_scale=col_scale is not None,
                             has_bias=bias is not None, act=act)
    return pl.pallas_call(
        body,
        grid=(m // tm, n // tn),
        in_specs=specs,
        out_specs=pl.BlockSpec((tm, tn), lambda i, j: (i, j)),
        out_shape=jax.ShapeDtypeStruct((m, n), out_dtype),
        scratch_shapes=[pltpu.VMEM((tm, k), BF16)],
        compiler_params=_params("parallel", "arbitrary"),
    )(*args)


def _sb_attn_body(q_ref, k_ref, v_ref, o_ref, acc_ref, run_ref, *, t, group):
    i = pl.program_id(1)
    dh = SB_HEAD_DIM
    row = lax.broadcasted_iota(I32, (t, t), 0)
    col = lax.broadcasted_iota(I32, (t, t), 1)
    causal = col < row
    later_keys = (row > col).astype(BF16)
    later_keys = jnp.concatenate([later_keys, later_keys], axis=0)
    heads = range(group)
    head_cols = [slice(h * dh, (h + 1) * dh) for h in heads]

    def tile(kb, diagonal):
        start = pl.multiple_of(kb * t, t)
        z = [lax.dot_general(q_ref[:, c], k_ref[pl.ds(start, t), c], (((1,), (1,)), ((), ())),
                             preferred_element_type=F32) for c in head_cols]
        softplus = [jnp.maximum(x, 0.0) + jnp.log(1.0 + jnp.exp(-jnp.abs(x))) for x in z]
        neg_log_keep = [jnp.where(causal, x, 0.0) for x in softplus] if diagonal else softplus
        later = []
        for x in neg_log_keep:
            hi = x.astype(BF16)
            lo = (x - hi.astype(F32)).astype(BF16)
            later.append(jnp.dot(jnp.concatenate([hi, lo], axis=1), later_keys, preferred_element_type=F32))
        run_min = None
        for h in heads:
            if diagonal:
                w = jnp.where(causal, jnp.exp(z[h] - softplus[h] - later[h]), 0.0)
                run = jnp.sum(neg_log_keep[h], axis=1, keepdims=True)
            else:
                run = run_ref[h]
                w = jnp.exp(z[h] - softplus[h] - later[h] - run)
                run = run + jnp.sum(neg_log_keep[h], axis=1, keepdims=True)
            pv = jnp.dot(w.astype(BF16), v_ref[pl.ds(start, t), head_cols[h]], preferred_element_type=F32)
            if diagonal:
                acc_ref[h] = pv
            else:
                acc_ref[h] += pv
            run_ref[h] = run
            head_min = jnp.min(run)
            run_min = head_min if run_min is None else jnp.minimum(run_min, head_min)
        return run_min

    def cond(carry):
        kb, run_min = carry
        return jnp.logical_and(kb >= 0, run_min < SB_DEAD_LOG)

    def step(carry):
        kb, _ = carry
        return kb - 1, tile(kb, False)

    lax.while_loop(cond, step, (i - 1, tile(i, True)))
    for h in range(group):
        o_ref[:, h * dh:(h + 1) * dh] = acc_ref[h].astype(o_ref.dtype)


def _sb_attention(qkv, *, heads, t=256, group=4):
    s = qkv.shape[0]
    dh = SB_HEAD_DIM
    t, group = min(t, s), min(group, heads)
    assert s % t == 0 and heads % group == 0
    n_groups = heads // group
    width = group * dh
    return pl.pallas_call(
        functools.partial(_sb_attn_body, t=t, group=group),
        grid=(n_groups, s // t),
        in_specs=[
            pl.BlockSpec((t, width), lambda g, i: (i, g)),
            pl.BlockSpec((s, width), lambda g, i: (0, n_groups + g)),
            pl.BlockSpec((s, width), lambda g, i: (0, 2 * n_groups + g)),
        ],
        out_specs=pl.BlockSpec((t, width), lambda g, i: (i, g)),
        out_shape=jax.ShapeDtypeStruct((s, heads * dh), BF16),
        scratch_shapes=[pltpu.VMEM((group, t, dh), F32), pltpu.VMEM((group, t, 1), F32)],
        compiler_params=_params("parallel", "arbitrary"),
    )(qkv, qkv, qkv)


def _pool_body(p_ref, halo_ref, lin_ref, scale_ref, o_ref, *, ts):
    i = pl.program_id(0)
    cur = p_ref[...]
    halo = jnp.where(i > 0, halo_ref[...], 0.0)
    ext = jnp.concatenate([halo, cur], axis=0)
    pos = i * ts + lax.broadcasted_iota(I32, (ts, 1), 0)
    c = POOL_GROUP_DIM
    for g, window in enumerate(POOL_WINDOWS):
        total = ext[:, g * c:(g + 1) * c]
        span = 1
        while span < window:
            total = total + pltpu.roll(total, span, 0)
            span *= 2
        total = total[POOL_HALO:]
        count = jnp.minimum(pos + 1, window).astype(F32)
        pooled = total / count - cur[:, g * c:(g + 1) * c]
        mixed = jnp.dot(pooled.astype(BF16), lin_ref[g], preferred_element_type=F32)
        o_ref[:, g * c:(g + 1) * c] = (mixed * scale_ref[:, g * c:(g + 1) * c]).astype(o_ref.dtype)


def _pool(p, pool_lin, pool_scale, *, ts=512):
    s, width = p.shape
    ts = min(ts, s)
    assert s % ts == 0 and ts % POOL_HALO == 0
    per = ts // POOL_HALO
    groups = len(POOL_WINDOWS)
    return pl.pallas_call(
        functools.partial(_pool_body, ts=ts),
        grid=(s // ts,),
        in_specs=[
            pl.BlockSpec((ts, width), lambda i: (i, 0)),
            pl.BlockSpec((POOL_HALO, width), lambda i: (jnp.maximum(i * per - 1, 0), 0)),
            pl.BlockSpec((groups, POOL_GROUP_DIM, POOL_GROUP_DIM), lambda i: (0, 0, 0)),
            pl.BlockSpec((1, width), lambda i: (0, 0)),
        ],
        out_specs=pl.BlockSpec((ts, width), lambda i: (i, 0)),
        out_shape=jax.ShapeDtypeStruct((s, width), BF16),
        compiler_params=_params("parallel"),
    )(p, p, pool_lin, pool_scale.reshape(1, width).astype(F32))


def _merge_body(oa_ref, ob_ref, ga_ref, gb_ref, h_ref, wa_ref, wb_ref, wm_ref, o_ref):
    ya = jnp.dot(oa_ref[...], wa_ref[...], preferred_element_type=F32)
    yb = jnp.dot(ob_ref[...], wb_ref[...], preferred_element_type=F32)
    merged = ga_ref[...].astype(F32) * ya + gb_ref[...].astype(F32) * yb
    o_ref[...] = h_ref[...] + jnp.dot(merged.astype(BF16), wm_ref[...], preferred_element_type=F32)


def _const_spec(shape):
    return pl.BlockSpec(shape, lambda i: (0,) * len(shape), pipeline_mode=pl.Buffered(1))


def _merge(o_a, o_b, gates, h, w_a, w_b, w_mix, *, tm=256):
    s, d = h.shape
    tm = min(tm, s)
    return pl.pallas_call(
        _merge_body,
        grid=(s // tm,),
        in_specs=[
            pl.BlockSpec((tm, o_a.shape[1]), lambda i: (i, 0)),
            pl.BlockSpec((tm, o_b.shape[1]), lambda i: (i, 0)),
            pl.BlockSpec((tm, d), lambda i: (i, 0)),
            pl.BlockSpec((tm, d), lambda i: (i, 1)),
            pl.BlockSpec((tm, d), lambda i: (i, 0)),
            _const_spec(w_a.shape), _const_spec(w_b.shape), _const_spec(w_mix.shape),
        ],
        out_specs=pl.BlockSpec((tm, d), lambda i: (i, 0)),
        out_shape=jax.ShapeDtypeStruct((s, d), F32),
        compiler_params=_params("parallel"),
    )(o_a, o_b, gates, gates, h, w_a, w_b, w_mix)


def _xattn_router_body(h_ref, xg_ref, wq_ref, k_ref, v_ref, wo_ref, mg_ref, wr_ref, br_ref,
                       h2_ref, idx_ref, gate_ref):
    h = h_ref[...]
    u = _rms(h, xg_ref[...]).astype(BF16)
    q = (jnp.dot(u, wq_ref[...], preferred_element_type=F32) * (XA_HEAD_DIM ** -0.5)).astype(BF16)
    heads = []
    for hd in range(XA_HEADS):
        sl = slice(hd * XA_HEAD_DIM, (hd + 1) * XA_HEAD_DIM)
        s = lax.dot_general(q[:, sl], k_ref[:, sl], (((1,), (1,)), ((), ())), preferred_element_type=F32)
        p = jnp.exp(s - jnp.max(s, axis=-1, keepdims=True))
        o = jnp.dot(p.astype(BF16), v_ref[:, sl], preferred_element_type=F32)
        heads.append(o / jnp.sum(p, axis=-1, keepdims=True))
    o = jnp.concatenate(heads, axis=1).astype(BF16)
    h2 = h + jnp.dot(o, wo_ref[...], preferred_element_type=F32)
    h2_ref[...] = h2

    t = _rms(h2, mg_ref[...])
    t_hi = t.astype(BF16)
    t_lo = (t - t_hi.astype(F32)).astype(BF16)
    logits = (jnp.dot(t_hi, wr_ref[0], preferred_element_type=F32)
              + jnp.dot(t_hi, wr_ref[1], preferred_element_type=F32)
              + jnp.dot(t_lo, wr_ref[0], preferred_element_type=F32)) + br_ref[...]
    n_e = logits.shape[1]
    lane = lax.broadcasted_iota(I32, logits.shape, 1).astype(F32)
    vals, idxs = [], []
    for _ in range(TOP_K):
        best = jnp.max(logits, axis=-1, keepdims=True)
        arg = jnp.min(jnp.where(logits == best, lane, float(n_e)), axis=-1, keepdims=True)
        vals.append(best)
        idxs.append(arg)
        logits = jnp.where(lane == arg, -jnp.inf, logits)
    e = jnp.exp(jnp.concatenate(vals, axis=1) - vals[0])
    gate_ref[...] = e / jnp.sum(e, axis=-1, keepdims=True)
    idx_ref[...] = jnp.concatenate(idxs, axis=1).astype(I32)


def _xattn_router(h, xa_gain, wq, kx, vx, wo, moe_gain, w_router, b_router, *, tm=512):
    s, d = h.shape
    tm = min(tm, s)
    n_e = w_router.shape[1]
    w_hi = w_router.astype(BF16)
    w_router = jnp.stack([w_hi, (w_router - w_hi.astype(F32)).astype(BF16)])
    row = lambda width: pl.BlockSpec((tm, width), lambda i: (i, 0))
    return pl.pallas_call(
        _xattn_router_body,
        grid=(s // tm,),
        in_specs=[
            row(d), _const_spec((1, d)), _const_spec(wq.shape), _const_spec(kx.shape), _const_spec(vx.shape),
            _const_spec(wo.shape), _const_spec((1, d)), _const_spec(w_router.shape), _const_spec((1, n_e)),
        ],
        out_specs=[row(d), row(TOP_K), row(TOP_K)],
        out_shape=[
            jax.ShapeDtypeStruct((s, d), F32),
            jax.ShapeDtypeStruct((s, TOP_K), I32),
            jax.ShapeDtypeStruct((s, TOP_K), F32),
        ],
        compiler_params=_params("parallel"),
    )(h, xa_gain.reshape(1, d), wq, kx, vx, wo, moe_gain.reshape(1, d), w_router, b_router.reshape(1, n_e))


def _expert_body(blk_e_ref, blk_src_ref, grains_ref, off_ref, tok_ref,
                 h_hbm, gain_ref, wup_hbm, wdn_hbm, bup_ref, bdn_ref, o_ref,
                 xst_ref, xb_ref, wg_st, wl_st, wd_st, wgb_ref, wlb_ref, wdb_ref, wsem, xsem, *, nj, ft, n_blocks):
    del blk_src_ref
    b = pl.program_id(0)
    grains = grains_ref[b]
    nxt = jnp.minimum(b + 1, n_blocks - 1)
    has_next = jnp.logical_and(b + 1 < n_blocks, grains_ref[nxt] > 0)
    d = o_ref.shape[1]
    n_grains = EXPERT_ROWS // EXPERT_GRAIN
    chunk = EXPERT_ROWS // nj

    def weight_copies(blk, j, slot):
        e = blk_e_ref[blk]
        c = pl.multiple_of(j * ft, ft)
        return (
            pltpu.make_async_copy(wup_hbm.at[e, :, pl.ds(c, ft)], wg_st.at[slot], wsem.at[slot, 0]),
            pltpu.make_async_copy(wup_hbm.at[e, :, pl.ds(nj * ft + c, ft)], wl_st.at[slot], wsem.at[slot, 1]),
            pltpu.make_async_copy(wdn_hbm.at[e, pl.ds(c, ft), :], wd_st.at[slot], wsem.at[slot, 2]),
        )

    def cast_weights(slot):
        wgb_ref[slot] = wg_st[slot].astype(BF16)
        wlb_ref[slot] = wl_st[slot].astype(BF16)
        wdb_ref[slot] = wd_st[slot].astype(BF16)

    def row_copy(blk, c, r):
        row = c * chunk + r
        return pltpu.make_async_copy(h_hbm.at[pl.ds(tok_ref[off_ref[blk] + row], 1), :],
                                     xst_ref.at[pl.ds(row, 1), :], xsem.at[c])

    def drain_rows(blk):
        def drain(i, carry):
            row_copy(blk, i // chunk, i % chunk).wait()
            return carry

        lax.fori_loop(0, EXPERT_ROWS, drain, 0, unroll=8)

    @pl.when(b == 0)
    def _():
        for j in range(2):
            for copy in weight_copies(0, j, j):
                copy.start()

        def start(i, carry):
            row_copy(0, i // chunk, i % chunk).start()
            return carry

        lax.fori_loop(0, EXPERT_ROWS, start, 0, unroll=8)
        for copy in weight_copies(0, 0, 0):
            copy.wait()
        cast_weights(0)

    @pl.when(jnp.logical_and(grains == 0, grains_ref[jnp.maximum(b - 1, 0)] > 0))
    def _():
        drain_rows(b)

    @pl.when(grains > 0)
    def _():
        drain_rows(b)
        for q in range(n_grains):
            @pl.when(q < grains)
            def _():
                grain = pl.ds(q * EXPERT_GRAIN, EXPERT_GRAIN)
                xb_ref[grain, :] = _rms(xst_ref[grain, :], gain_ref[...]).astype(BF16)

        o_ref[...] = jnp.broadcast_to(bdn_ref[0], o_ref.shape)

        def hidden_tile(j, carry):
            slot = j % 2
            in_block_1, in_block_2 = j + 1 < nj, j + 2 < nj

            @pl.when(jnp.logical_or(in_block_1, has_next))
            def _():
                blk = jnp.where(in_block_1, b, nxt)
                for copy in weight_copies(blk, (j + 1) % nj, 1 - slot):
                    copy.wait()

            @pl.when(jnp.logical_or(in_block_2, has_next))
            def _():
                blk = jnp.where(in_block_2, b, nxt)
                for copy in weight_copies(blk, (j + 2) % nj, slot):
                    copy.start()

            bias_g = bup_ref[0, pl.ds(j, 1), :]
            bias_l = bup_ref[0, pl.ds(nj + j, 1), :]
            for m_grains in range(1, n_grains + 1):
                @pl.when(grains == m_grains)
                def _():
                    m = m_grains * EXPERT_GRAIN
                    x = xb_ref[0:m, :]
                    down_chunks = range(0, d, EXPERT_DOWN_CHUNK)
                    pieces = 2 + len(down_chunks)
                    bounds = [chunk * p // pieces for p in range(pieces + 1)]

                    def start_piece(p):
                        for r in range(bounds[p], bounds[p + 1]):
                            row_copy(nxt, j, r).start()

                    start_piece(0)
                    hg = jnp.dot(x, wgb_ref[slot], preferred_element_type=F32) + bias_g
                    start_piece(1)
                    hl = jnp.dot(x, wlb_ref[slot], preferred_element_type=F32) + bias_l
                    cast_weights(1 - slot)
                    g = jnp.minimum(hg, SWIGLU_LIMIT)
                    l = jnp.clip(hl, -SWIGLU_LIMIT, SWIGLU_LIMIT)
                    act = (g * jax.nn.sigmoid(SWIGLU_ALPHA * g) * (l + 1.0)).astype(BF16)
                    for p, n in enumerate(down_chunks):
                        cols = slice(n, min(n + EXPERT_DOWN_CHUNK, d))
                        start_piece(2 + p)
                        o_ref[0:m, cols] += jnp.dot(act, wdb_ref[slot, :, cols], preferred_element_type=F32)
            return carry

        lax.fori_loop(0, nj, hidden_tile, 0)

        @pl.when(b == n_blocks - 1)
        def _():
            drain_rows(b)


def _experts(h, gain, sorted_tok, blk_expert, blk_src, blk_grains, blk_off, w_up, b_up, w_down, b_down):
    d = h.shape[1]
    n_e, _, two_f = w_up.shape
    f = two_f // 2
    ft = min(EXPERT_FT, f)
    nj = f // ft
    assert nj % 2 == 0
    nb = blk_expert.shape[0]
    n_grains = EXPERT_ROWS // EXPERT_GRAIN
    by_expert = lambda b, be, bs, gr, off, tok: (be[b], 0, 0)
    return pl.pallas_call(
        functools.partial(_expert_body, nj=nj, ft=ft, n_blocks=nb),
        grid_spec=pltpu.PrefetchScalarGridSpec(
            num_scalar_prefetch=5,
            grid=(nb,),
            in_specs=[
                pl.BlockSpec(memory_space=pl.ANY),
                pl.BlockSpec((1, d), lambda b, *_: (0, 0)),
                pl.BlockSpec(memory_space=pl.ANY),
                pl.BlockSpec(memory_space=pl.ANY),
                pl.BlockSpec((1, 2 * nj, ft), by_expert),
                pl.BlockSpec((1, 1, d), by_expert),
            ],
            out_specs=pl.BlockSpec((EXPERT_ROWS, d), lambda b, be, bs, gr, off, tok: (bs[b], 0)),
            scratch_shapes=[
                pltpu.VMEM((EXPERT_ROWS, d), F32), pltpu.VMEM((EXPERT_ROWS, d), BF16),
                pltpu.VMEM((2, d, ft), F32), pltpu.VMEM((2, d, ft), F32), pltpu.VMEM((2, ft, d), F32),
                pltpu.VMEM((2, d, ft), BF16), pltpu.VMEM((2, d, ft), BF16), pltpu.VMEM((2, ft, d), BF16),
                pltpu.SemaphoreType.DMA((2, 3)), pltpu.SemaphoreType.DMA((nj,)),
            ],
        ),
        out_shape=jax.ShapeDtypeStruct((nb * EXPERT_ROWS, d), F32),
        compiler_params=_params("arbitrary", vmem_limit_bytes=EXPERT_VMEM_LIMIT_BYTES),
    )(blk_expert, blk_src, blk_grains, blk_off, sorted_tok,
      h, gain.reshape(1, d), w_up, w_down, b_up.reshape(n_e, 2 * nj, ft), b_down.reshape(n_e, 1, d))


def _combine_body(dest_ref, ys_hbm, gate_ref, h_ref, fg_ref, o_ref, buf_ref, sem, *, tt, n_tiles):
    i = pl.program_id(0)

    def row_copy(tile, t, k):
        slot = tile % 2
        return pltpu.make_async_copy(ys_hbm.at[pl.ds(dest_ref[(tile * tt + t) * TOP_K + k], 1), :],
                                     buf_ref.at[slot, k, pl.ds(t, 1), :], sem.at[slot])

    @pl.when(i < n_tiles)
    def _():
        for t in range(tt):
            for k in range(TOP_K):
                row_copy(i, t, k).start()

    @pl.when(i >= 1)
    def _():
        def drain(t, carry):
            for k in range(TOP_K):
                row_copy(i - 1, t, k).wait()
            return carry

        lax.fori_loop(0, tt, drain, 0, unroll=4)
        slot = (i - 1) % 2
        gate = gate_ref[...]
        y = h_ref[...]
        for k in range(TOP_K):
            y = y + gate[:, k:k + 1] * buf_ref[slot, k]
        o_ref[...] = _rms(y, fg_ref[...])


def _combine(ys, dest, gate, h, final_gain, *, tt=128):
    n, d = h.shape
    tt = min(tt, n)
    n_tiles = n // tt
    prev = lambda i, dst: (jnp.maximum(i - 1, 0), 0)
    return pl.pallas_call(
        functools.partial(_combine_body, tt=tt, n_tiles=n_tiles),
        grid_spec=pltpu.PrefetchScalarGridSpec(
            num_scalar_prefetch=1,
            grid=(n_tiles + 1,),
            in_specs=[
                pl.BlockSpec(memory_space=pl.ANY),
                pl.BlockSpec((tt, TOP_K), prev),
                pl.BlockSpec((tt, d), prev),
                pl.BlockSpec((1, d), lambda i, dst: (0, 0)),
            ],
            out_specs=pl.BlockSpec((tt, d), prev),
            scratch_shapes=[pltpu.VMEM((2, TOP_K, tt, d), F32), pltpu.SemaphoreType.DMA((2,))],
        ),
        out_shape=jax.ShapeDtypeStruct((n, d), F32),
        compiler_params=_params("arbitrary"),
    )(dest, ys, gate, h, final_gain.reshape(1, d))


def _routing_tables(top_idx, n_blocks):
    flat_e = top_idx.reshape(-1)
    nk = flat_e.shape[0]
    ids = jnp.arange(nk, dtype=I32)
    experts = jnp.arange(N_EXPERTS, dtype=I32)
    counts = jnp.sum((flat_e[:, None] == experts[None, :]).astype(I32), axis=0)
    first = jnp.cumsum(counts) - counts
    blocks_per_e = (counts + EXPERT_ROWS - 1) // EXPERT_ROWS
    blk_end = jnp.cumsum(blocks_per_e)
    blk_start = blk_end - blocks_per_e

    order = jnp.sort(flat_e * nk + ids)
    sorted_id = order % nk
    shift = blk_start * EXPERT_ROWS - first
    sorted_dest = ids + jnp.take(shift, order // nk)
    _, dest = lax.sort_key_val(sorted_id, sorted_dest)
    sorted_tok = jnp.concatenate([sorted_id // TOP_K, jnp.arange(EXPERT_ROWS, dtype=I32) % (nk // TOP_K)])

    b = jnp.arange(n_blocks, dtype=I32)
    used = blk_end[-1]
    src = jnp.minimum(b, used - 1)
    blk_expert = jnp.minimum(jnp.sum((blk_end[None, :] <= src[:, None]).astype(I32), axis=1), N_EXPERTS - 1)
    row0 = (src - blk_start[blk_expert]) * EXPERT_ROWS
    rows_here = jnp.clip(counts[blk_expert] - row0, 0, EXPERT_ROWS)
    grains = jnp.where(b < used, (rows_here + EXPERT_GRAIN - 1) // EXPERT_GRAIN, 0)
    blk_off = first[blk_expert] + row0
    i32 = lambda a: a.astype(I32)
    return i32(dest), i32(sorted_tok), i32(blk_expert), i32(src), i32(grains), i32(blk_off)


def kernel(x, mem, norm_mix, w_in, pool_lin, pool_scale, w_branch_a, w_branch_b, w_gate, b_gate, w_mix_out,
           xa_norm, mem_norm, xa_wq, xa_wkv, xa_wo, moe_norm, w_router, b_router, w_up, b_up, w_down, b_down,
           final_norm):
    batch, s, d = x.shape
    assert batch == 1 and norm_mix.shape[0] == 1
    h = x.reshape(s, d)
    l = 0
    bf = lambda w: w.astype(BF16)

    qkv_w = 3 * SB_WIDTH
    q_scale = jnp.concatenate([jnp.full((SB_WIDTH,), SB_HEAD_DIM ** -0.5, F32), jnp.ones((2 * SB_WIDTH,), F32)])
    qkv = _linear(h, bf(w_in[l][:, :qkv_w]), gain=norm_mix[l], col_scale=q_scale, out_dtype=BF16)
    p = _linear(h, bf(w_in[l][:, qkv_w:]), gain=norm_mix[l], out_dtype=F32)
    gates = _linear(h, bf(w_gate[l]), gain=norm_mix[l], bias=b_gate[l], act="sigmoid", out_dtype=BF16)
    o_a = _sb_attention(qkv, heads=SB_HEADS)
    o_b = _pool(p, bf(pool_lin[l]), pool_scale[l])
    h = _merge(o_a, o_b, gates, h, bf(w_branch_a[l]), bf(w_branch_b[l]), bf(w_mix_out[l]))

    kv = _linear(mem.reshape(mem.shape[1], d), bf(xa_wkv[l]), gain=mem_norm[l], out_dtype=BF16)
    h, top_idx, gate = _xattn_router(h, xa_norm[l], bf(xa_wq[l]), kv[:, :XA_WIDTH], kv[:, XA_WIDTH:],
                                     bf(xa_wo[l]), moe_norm[l], w_router[l], b_router[l])

    n_blocks = (s * TOP_K) // EXPERT_ROWS + N_EXPERTS
    dest, sorted_tok, blk_expert, blk_src, blk_grains, blk_off = _routing_tables(top_idx, n_blocks)
    ys = _experts(h, moe_norm[l], sorted_tok, blk_expert, blk_src, blk_grains, blk_off,
                  w_up[l], b_up[l], w_down[l], b_down[l])
    out = _combine(ys, dest, gate, h, final_norm)
    return out.reshape(batch, s, d)
```

```python
import functools
import math

import jax
import jax.numpy as jnp
from jax import lax
from jax.experimental import pallas as pl
from jax.experimental.pallas import tpu as pltpu

F32 = jnp.float32
BF16 = jnp.bfloat16
U32 = jnp.uint32
I32 = jnp.int32

RMS_EPS = 1e-5

SB_HEADS = 8
SB_HEAD_DIM = 128
SB_WIDTH = SB_HEADS * SB_HEAD_DIM
POOL_WINDOWS = (2, 4, 8, 16)
POOL_GROUP_DIM = 256
POOL_WIDTH = len(POOL_WINDOWS) * POOL_GROUP_DIM
POOL_HALO = 16
XA_HEADS = 4
XA_HEAD_DIM = 128
XA_WIDTH = XA_HEADS * XA_HEAD_DIM
N_EXPERTS = 32
TOP_K = 4
SWIGLU_ALPHA = 1.702
SWIGLU_LIMIT = 7.0

VMEM_LIMIT_BYTES = 56 * 1024 * 1024
EXPERT_VMEM_LIMIT_BYTES = 60 * 1024 * 1024

SB_DEAD_LOG = 110.0

EXPERT_ROWS = 1280
EXPERT_GRAIN = 128
EXPERT_FT = 256
EXPERT_DOWN_CHUNK = 512


def _params(*sem, vmem_limit_bytes=VMEM_LIMIT_BYTES):
    return pltpu.CompilerParams(dimension_semantics=sem, vmem_limit_bytes=vmem_limit_bytes)


def _rms(x, gain):
    return x * lax.rsqrt(jnp.mean(x * x, axis=-1, keepdims=True) + RMS_EPS) * gain


def _linear_body(*refs, rms, has_scale, has_bias, act):
    refs = list(refs)
    a_ref = refs.pop(0)
    g_ref = refs.pop(0) if rms else None
    w_ref = refs.pop(0)
    s_ref = refs.pop(0) if has_scale else None
    b_ref = refs.pop(0) if has_bias else None
    o_ref, abf_ref = refs

    @pl.when(pl.program_id(1) == 0)
    def _():
        a = a_ref[...].astype(F32)
        if rms:
            a = _rms(a, g_ref[...])
        abf_ref[...] = a.astype(BF16)

    acc = jnp.dot(abf_ref[...], w_ref[...].astype(BF16), preferred_element_type=F32)
    if has_scale:
        acc = acc * s_ref[...]
    if has_bias:
        acc = acc + b_ref[...]
    if act == "sigmoid":
        acc = jax.nn.sigmoid(acc)
    o_ref[...] = acc.astype(o_ref.dtype)


def _linear(a, w, *, cols=None, gain=None, col_scale=None, bias=None, act=None, out_dtype=F32, tm=1024, tn=1024):
    m, k = a.shape
    first, n = cols if cols is not None else (0, w.shape[1])
    tm, tn = min(tm, m), min(tn, math.gcd(n, first))
    assert m % tm == 0 and n % tn == 0 and first % tn == 0
    first_block = first // tn
    args, specs = [a], [pl.BlockSpec((tm, k), lambda i, j: (i, 0))]
    if gain is not None:
        args.append(gain.reshape(1, k).astype(F32))
        specs.append(pl.BlockSpec((1, k), lambda i, j: (0, 0)))
    args.append(w)
    specs.append(pl.BlockSpec((k, tn), lambda i, j: (0, first_block + j)))
    for vec in (col_scale, bias):
        if vec is not None:
            args.append(vec.reshape(1, n).astype(F32))
            specs.append(pl.BlockSpec((1, tn), lambda i, j: (0, j)))
    body = functools.partial(_linear_body, rms=gain is not None, has_scale=col_scale is not None,
                             has_bias=bias is not None, act=act)
    return pl.pallas_call(
        body,
        grid=(m // tm, n // tn),
        in_specs=specs,
        out_specs=pl.BlockSpec((tm, tn), lambda i, j: (i, j)),
        out_shape=jax.ShapeDtypeStruct((m, n), out_dtype),
        scratch_shapes=[pltpu.VMEM((tm, k), BF16)],
        compiler_params=_params("parallel", "arbitrary"),
    )(*args)


def _sb_attn_body(q_ref, k_ref, v_ref, o_ref, acc_ref, run_ref, *, t, group):
    i = pl.program_id(1)
    dh = SB_HEAD_DIM
    row = lax.broadcasted_iota(I32, (t, t), 0)
    col = lax.broadcasted_iota(I32, (t, t), 1)
    causal = col < row
    later_keys = (row > col).astype(BF16)
    later_keys = jnp.concatenate([later_keys, later_keys], axis=0)
    heads = range(group)
    head_cols = [slice(h * dh, (h + 1) * dh) for h in heads]

    def tile(kb, diagonal):
        start = pl.multiple_of(kb * t, t)
        z = [lax.dot_general(q_ref[:, c], k_ref[pl.ds(start, t), c], (((1,), (1,)), ((), ())),
                             preferred_element_type=F32) for c in head_cols]
        softplus = [jnp.maximum(x, 0.0) + jnp.log(1.0 + jnp.exp(-jnp.abs(x))) for x in z]
        neg_log_keep = [jnp.where(causal, x, 0.0) for x in softplus] if diagonal else softplus
        later = []
        for x in neg_log_keep:
            hi = x.astype(BF16)
            lo = (x - hi.astype(F32)).astype(BF16)
            later.append(jnp.dot(jnp.concatenate([hi, lo], axis=1), later_keys, preferred_element_type=F32))
        run_min = None
        for h in heads:
            if diagonal:
                w = jnp.where(causal, jnp.exp(z[h] - softplus[h] - later[h]), 0.0)
                run = jnp.sum(neg_log_keep[h], axis=1, keepdims=True)
            else:
                run = run_ref[h]
                w = jnp.exp(z[h] - softplus[h] - later[h] - run)
                run = run + jnp.sum(neg_log_keep[h], axis=1, keepdims=True)
            pv = jnp.dot(w.astype(BF16), v_ref[pl.ds(start, t), head_cols[h]], preferred_element_type=F32)
            if diagonal:
                acc_ref[h] = pv
            else:
                acc_ref[h] += pv
            run_ref[h] = run
            head_min = jnp.min(run)
            run_min = head_min if run_min is None else jnp.minimum(run_min, head_min)
        return run_min

    def cond(carry):
        kb, run_min = carry
        return jnp.logical_and(kb >= 0, run_min < SB_DEAD_LOG)

    def step(carry):
        kb, _ = carry
        return kb - 1, tile(kb, False)

    lax.while_loop(cond, step, (i - 1, tile(i, True)))
    for h in range(group):
        o_ref[:, h * dh:(h + 1) * dh] = acc_ref[h].astype(o_ref.dtype)


def _sb_attention(qkv, *, heads, t=256, group=4):
    s = qkv.shape[0]
    dh = SB_HEAD_DIM
    t, group = min(t, s), min(group, heads)
    assert s % t == 0 and heads % group == 0
    n_groups = heads // group
    width = group * dh
    return pl.pallas_call(
        functools.partial(_sb_attn_body, t=t, group=group),
        grid=(n_groups, s // t),
        in_specs=[
            pl.BlockSpec((t, width), lambda g, i: (i, g)),
            pl.BlockSpec((s, width), lambda g, i: (0, n_groups + g)),
            pl.BlockSpec((s, width), lambda g, i: (0, 2 * n_groups + g)),
        ],
        out_specs=pl.BlockSpec((t, width), lambda g, i: (i, g)),
        out_shape=jax.ShapeDtypeStruct((s, heads * dh), BF16),
        scratch_shapes=[pltpu.VMEM((group, t, dh), F32), pltpu.VMEM((group, t, 1), F32)],
        compiler_params=_params("parallel", "arbitrary"),
    )(qkv, qkv, qkv)


def _pool_body(p_ref, halo_ref, lin_ref, scale_ref, o_ref, *, ts):
    i = pl.program_id(0)
    cur = p_ref[...]
    halo = jnp.where(i > 0, halo_ref[...], 0.0)
    ext = jnp.concatenate([halo, cur], axis=0)
    pos = i * ts + lax.broadcasted_iota(I32, (ts, 1), 0)
    c = POOL_GROUP_DIM
    for g, window in enumerate(POOL_WINDOWS):
        total = ext[:, g * c:(g + 1) * c]
        span = 1
        while span < window:
            total = total + pltpu.roll(total, span, 0)
            span *= 2
        total = total[POOL_HALO:]
        count = jnp.minimum(pos + 1, window).astype(F32)
        pooled = total / count - cur[:, g * c:(g + 1) * c]
        mixed = jnp.dot(pooled.astype(BF16), lin_ref[g], preferred_element_type=F32)
        o_ref[:, g * c:(g + 1) * c] = (mixed * scale_ref[:, g * c:(g + 1) * c]).astype(o_ref.dtype)


def _pool(p, pool_lin, pool_scale, *, ts=512):
    s, width = p.shape
    ts = min(ts, s)
    assert s % ts == 0 and ts % POOL_HALO == 0
    per = ts // POOL_HALO
    groups = len(POOL_WINDOWS)
    return pl.pallas_call(
        functools.partial(_pool_body, ts=ts),
        grid=(s // ts,),
        in_specs=[
            pl.BlockSpec((ts, width), lambda i: (i, 0)),
            pl.BlockSpec((POOL_HALO, width), lambda i: (jnp.maximum(i * per - 1, 0), 0)),
            pl.BlockSpec((groups, POOL_GROUP_DIM, POOL_GROUP_DIM), lambda i: (0, 0, 0)),
            pl.BlockSpec((1, width), lambda i: (0, 0)),
        ],
        out_specs=pl.BlockSpec((ts, width), lambda i: (i, 0)),
        out_shape=jax.ShapeDtypeStruct((s, width), BF16),
        compiler_params=_params("parallel"),
    )(p, p, pool_lin, pool_scale.reshape(1, width).astype(F32))


def _merge_body(oa_ref, ob_ref, ga_ref, gb_ref, h_ref, wa_ref, wb_ref, wm_ref, o_ref):
    ya = jnp.dot(oa_ref[...], wa_ref[...], preferred_element_type=F32)
    yb = jnp.dot(ob_ref[...], wb_ref[...], preferred_element_type=F32)
    merged = ga_ref[...].astype(F32) * ya + gb_ref[...].astype(F32) * yb
    o_ref[...] = h_ref[...] + jnp.dot(merged.astype(BF16), wm_ref[...], preferred_element_type=F32)


def _const_spec(shape):
    return pl.BlockSpec(shape, lambda i: (0,) * len(shape), pipeline_mode=pl.Buffered(1))


def _merge(o_a, o_b, gates, h, w_a, w_b, w_mix, *, tm=256):
    s, d = h.shape
    tm = min(tm, s)
    return pl.pallas_call(
        _merge_body,
        grid=(s // tm,),
        in_specs=[
            pl.BlockSpec((tm, o_a.shape[1]), lambda i: (i, 0)),
            pl.BlockSpec((tm, o_b.shape[1]), lambda i: (i, 0)),
            pl.BlockSpec((tm, d), lambda i: (i, 0)),
            pl.BlockSpec((tm, d), lambda i: (i, 1)),
            pl.BlockSpec((tm, d), lambda i: (i, 0)),
            _const_spec(w_a.shape), _const_spec(w_b.shape), _const_spec(w_mix.shape),
        ],
        out_specs=pl.BlockSpec((tm, d), lambda i: (i, 0)),
        out_shape=jax.ShapeDtypeStruct((s, d), F32),
        compiler_params=_params("parallel"),
    )(o_a, o_b, gates, gates, h, w_a, w_b, w_mix)


def _xattn_router_body(h_ref, xg_ref, wq_ref, k_ref, v_ref, wo_ref, mg_ref, wr_ref, br_ref,
                       h2_ref, idx_ref, gate_ref):
    h = h_ref[...]
    u = _rms(h, xg_ref[...]).astype(BF16)
    q = (jnp.dot(u, wq_ref[...], preferred_element_type=F32) * (XA_HEAD_DIM ** -0.5)).astype(BF16)
    heads = []
    for hd in range(XA_HEADS):
        sl = slice(hd * XA_HEAD_DIM, (hd + 1) * XA_HEAD_DIM)
        s = lax.dot_general(q[:, sl], k_ref[:, sl], (((1,), (1,)), ((), ())), preferred_element_type=F32)
        p = jnp.exp(s - jnp.max(s, axis=-1, keepdims=True))
        o = jnp.dot(p.astype(BF16), v_ref[:, sl], preferred_element_type=F32)
        heads.append(o / jnp.sum(p, axis=-1, keepdims=True))
    o = jnp.concatenate(heads, axis=1).astype(BF16)
    h2 = h + jnp.dot(o, wo_ref[...], preferred_element_type=F32)
    h2_ref[...] = h2

    t = _rms(h2, mg_ref[...])
    t_hi = t.astype(BF16)
    t_lo = (t - t_hi.astype(F32)).astype(BF16)
    logits = (jnp.dot(t_hi, wr_ref[0], preferred_element_type=F32)
              + jnp.dot(t_hi, wr_ref[1], preferred_element_type=F32)
              + jnp.dot(t_lo, wr_ref[0], preferred_element_type=F32)) + br_ref[...]
    n_e = logits.shape[1]
    lane = lax.broadcasted_iota(I32, logits.shape, 1).astype(F32)
    vals, idxs = [], []
    for _ in range(TOP_K):
        best = jnp.max(logits, axis=-1, keepdims=True)
        arg = jnp.min(jnp.where(logits == best, lane, float(n_e)), axis=-1, keepdims=True)
        vals.append(best)
        idxs.append(arg)
        logits = jnp.where(lane == arg, -jnp.inf, logits)
    e = jnp.exp(jnp.concatenate(vals, axis=1) - vals[0])
    gate_ref[...] = e / jnp.sum(e, axis=-1, keepdims=True)
    idx_ref[...] = jnp.concatenate(idxs, axis=1).astype(I32)


def _xattn_router(h, xa_gain, wq, kx, vx, wo, moe_gain, w_router, b_router, *, tm=512):
    s, d = h.shape
    tm = min(tm, s)
    n_e = w_router.shape[1]
    w_hi = w_router.astype(BF16)
    w_router = jnp.stack([w_hi, (w_router - w_hi.astype(F32)).astype(BF16)])
    row = lambda width: pl.BlockSpec((tm, width), lambda i: (i, 0))
    return pl.pallas_call(
        _xattn_router_body,
        grid=(s // tm,),
        in_specs=[
            row(d), _const_spec((1, d)), _const_spec(wq.shape), _const_spec(kx.shape), _const_spec(vx.shape),
            _const_spec(wo.shape), _const_spec((1, d)), _const_spec(w_router.shape), _const_spec((1, n_e)),
        ],
        out_specs=[row(d), row(TOP_K), row(TOP_K)],
        out_shape=[
            jax.ShapeDtypeStruct((s, d), F32),
            jax.ShapeDtypeStruct((s, TOP_K), I32),
            jax.ShapeDtypeStruct((s, TOP_K), F32),
        ],
        compiler_params=_params("parallel"),
    )(h, xa_gain.reshape(1, d), wq, kx, vx, wo, moe_gain.reshape(1, d), w_router, b_router.reshape(1, n_e))


def _expert_body(blk_e_ref, blk_src_ref, grains_ref, off_ref, tok_ref,
                 h_hbm, gain_ref, wup_hbm, wdn_hbm, bup_ref, bdn_ref, o_ref,
                 xst_ref, xb_ref, wg_st, wl_st, wd_st, wgb_ref, wlb_ref, wdb_ref, wsem, xsem, *, nj, ft, n_blocks):
    del blk_src_ref
    b = pl.program_id(0)
    grains = grains_ref[b]
    nxt = jnp.minimum(b + 1, n_blocks - 1)
    has_next = jnp.logical_and(b + 1 < n_blocks, grains_ref[nxt] > 0)
    d = o_ref.shape[1]
    n_grains = EXPERT_ROWS // EXPERT_GRAIN

    def weight_copies(blk, j, slot):
        e = blk_e_ref[blk]
        c = pl.multiple_of(j * ft, ft)
        return (
            pltpu.make_async_copy(wup_hbm.at[e, :, pl.ds(c, ft)], wg_st.at[slot], wsem.at[slot, 0]),
            pltpu.make_async_copy(wup_hbm.at[e, :, pl.ds(nj * ft + c, ft)], wl_st.at[slot], wsem.at[slot, 1]),
            pltpu.make_async_copy(wdn_hbm.at[e, pl.ds(c, ft), :], wd_st.at[slot], wsem.at[slot, 2]),
        )

    def cast_weights(slot):
        wgb_ref[slot] = wg_st[slot].astype(BF16)
        wlb_ref[slot] = wl_st[slot].astype(BF16)
        wdb_ref[slot] = wd_st[slot].astype(BF16)

    def row_copy(blk, r):
        return pltpu.make_async_copy(h_hbm.at[pl.ds(tok_ref[off_ref[blk] + r], 1), :],
                                     xst_ref.at[pl.ds(r, 1), :], xsem.at[r // EXPERT_GRAIN])

    @pl.when(b == 0)
    def _():
        for j in range(2):
            for copy in weight_copies(0, j, j):
                copy.start()

        def start(r, carry):
            row_copy(0, r).start()
            return carry

        lax.fori_loop(0, grains * EXPERT_GRAIN, start, 0)
        for copy in weight_copies(0, 0, 0):
            copy.wait()
        cast_weights(0)

    @pl.when(grains > 0)
    def _():
        def drain(r, carry):
            row_copy(b, r).wait()
            return carry

        lax.fori_loop(0, grains * EXPERT_GRAIN, drain, 0)
        for q in range(n_grains):
            @pl.when(q < grains)
            def _():
                grain = pl.ds(q * EXPERT_GRAIN, EXPERT_GRAIN)
                xb_ref[grain, :] = _rms(xst_ref[grain, :], gain_ref[...]).astype(BF16)

        o_ref[...] = jnp.broadcast_to(bdn_ref[0], o_ref.shape)

        def hidden_tile(j, carry):
            slot = j % 2
            in_block_1, in_block_2 = j + 1 < nj, j + 2 < nj

            @pl.when(jnp.logical_or(in_block_1, has_next))
            def _():
                blk = jnp.where(in_block_1, b, nxt)
                for copy in weight_copies(blk, (j + 1) % nj, 1 - slot):
                    copy.wait()

            @pl.when(jnp.logical_or(in_block_2, has_next))
            def _():
                blk = jnp.where(in_block_2, b, nxt)
                for copy in weight_copies(blk, (j + 2) % nj, slot):
                    copy.start()

            for q in range(n_grains):
                @pl.when(jnp.logical_and(jnp.logical_and(has_next, q < grains_ref[nxt]), j == q % nj))
                def _():
                    for r in range(q * EXPERT_GRAIN, (q + 1) * EXPERT_GRAIN):
                        row_copy(nxt, r).start()

            bias_g = bup_ref[0, pl.ds(j, 1), :]
            bias_l = bup_ref[0, pl.ds(nj + j, 1), :]
            for m_grains in range(1, n_grains + 1):
                @pl.when(grains == m_grains)
                def _():
                    m = m_grains * EXPERT_GRAIN
                    x = xb_ref[0:m, :]
                    hg = jnp.dot(x, wgb_ref[slot], preferred_element_type=F32) + bias_g
                    hl = jnp.dot(x, wlb_ref[slot], preferred_element_type=F32) + bias_l
                    cast_weights(1 - slot)
                    g = jnp.minimum(hg, SWIGLU_LIMIT)
                    l = jnp.clip(hl, -SWIGLU_LIMIT, SWIGLU_LIMIT)
                    act = (g * jax.nn.sigmoid(SWIGLU_ALPHA * g) * (l + 1.0)).astype(BF16)
                    for n in range(0, d, EXPERT_DOWN_CHUNK):
                        cols = slice(n, min(n + EXPERT_DOWN_CHUNK, d))
                        o_ref[0:m, cols] += jnp.dot(act, wdb_ref[slot, :, cols], preferred_element_type=F32)
            return carry

        lax.fori_loop(0, nj, hidden_tile, 0)


def _experts(h, gain, sorted_tok, blk_expert, blk_src, blk_grains, blk_off, w_up, b_up, w_down, b_down):
    d = h.shape[1]
    n_e, _, two_f = w_up.shape
    f = two_f // 2
    ft = min(EXPERT_FT, f)
    nj = f // ft
    assert nj % 2 == 0
    nb = blk_expert.shape[0]
    n_grains = EXPERT_ROWS // EXPERT_GRAIN
    by_expert = lambda b, be, bs, gr, off, tok: (be[b], 0, 0)
    return pl.pallas_call(
        functools.partial(_expert_body, nj=nj, ft=ft, n_blocks=nb),
        grid_spec=pltpu.PrefetchScalarGridSpec(
            num_scalar_prefetch=5,
            grid=(nb,),
            in_specs=[
                pl.BlockSpec(memory_space=pl.ANY),
                pl.BlockSpec((1, d), lambda b, *_: (0, 0)),
                pl.BlockSpec(memory_space=pl.ANY),
                pl.BlockSpec(memory_space=pl.ANY),
                pl.BlockSpec((1, 2 * nj, ft), by_expert),
                pl.BlockSpec((1, 1, d), by_expert),
            ],
            out_specs=pl.BlockSpec((EXPERT_ROWS, d), lambda b, be, bs, gr, off, tok: (bs[b], 0)),
            scratch_shapes=[
                pltpu.VMEM((EXPERT_ROWS, d), F32), pltpu.VMEM((EXPERT_ROWS, d), BF16),
                pltpu.VMEM((2, d, ft), F32), pltpu.VMEM((2, d, ft), F32), pltpu.VMEM((2, ft, d), F32),
                pltpu.VMEM((2, d, ft), BF16), pltpu.VMEM((2, d, ft), BF16), pltpu.VMEM((2, ft, d), BF16),
                pltpu.SemaphoreType.DMA((2, 3)), pltpu.SemaphoreType.DMA((n_grains,)),
            ],
        ),
        out_shape=jax.ShapeDtypeStruct((nb * EXPERT_ROWS, d), F32),
        compiler_params=_params("arbitrary", vmem_limit_bytes=EXPERT_VMEM_LIMIT_BYTES),
    )(blk_expert, blk_src, blk_grains, blk_off, sorted_tok,
      h, gain.reshape(1, d), w_up, w_down, b_up.reshape(n_e, 2 * nj, ft), b_down.reshape(n_e, 1, d))


def _combine_body(dest_ref, ys_hbm, gate_ref, h_ref, fg_ref, o_ref, buf_ref, sem, *, tt, n_tiles):
    i = pl.program_id(0)

    def row_copy(tile, slot, t, k):
        return pltpu.make_async_copy(ys_hbm.at[pl.ds(dest_ref[(tile * tt + t) * TOP_K + k], 1), :],
                                     buf_ref.at[slot, k, pl.ds(t, 1), :], sem.at[slot])

    for slot in range(2):
        @pl.when(jnp.logical_and(i < n_tiles, i % 2 == slot))
        def _():
            for t in range(tt):
                for k in range(TOP_K):
                    row_copy(i, slot, t, k).start()

    @pl.when(i >= 1)
    def _():
        slot = (i - 1) % 2

        def drain(t, carry):
            for k in range(TOP_K):
                row_copy(i - 1, slot, t, k).wait()
            return carry

        lax.fori_loop(0, tt, drain, 0, unroll=4)
        gate = gate_ref[...]
        y = h_ref[...]
        for k in range(TOP_K):
            y = y + gate[:, k:k + 1] * buf_ref[slot, k]
        o_ref[...] = _rms(y, fg_ref[...])


def _combine(ys, dest, gate, h, final_gain, *, tt=128):
    n, d = h.shape
    tt = min(tt, n)
    n_tiles = n // tt
    prev = lambda i, dst: (jnp.maximum(i - 1, 0), 0)
    return pl.pallas_call(
        functools.partial(_combine_body, tt=tt, n_tiles=n_tiles),
        grid_spec=pltpu.PrefetchScalarGridSpec(
            num_scalar_prefetch=1,
            grid=(n_tiles + 1,),
            in_specs=[
                pl.BlockSpec(memory_space=pl.ANY),
                pl.BlockSpec((tt, TOP_K), prev),
                pl.BlockSpec((tt, d), prev),
                pl.BlockSpec((1, d), lambda i, dst: (0, 0)),
            ],
            out_specs=pl.BlockSpec((tt, d), prev),
            scratch_shapes=[pltpu.VMEM((2, TOP_K, tt, d), F32), pltpu.SemaphoreType.DMA((2,))],
        ),
        out_shape=jax.ShapeDtypeStruct((n, d), F32),
        compiler_params=_params("arbitrary"),
    )(dest, ys, gate, h, final_gain.reshape(1, d))


def _routing_tables(top_idx, n_blocks):
    flat_e = top_idx.reshape(-1)
    nk = flat_e.shape[0]
    ids = jnp.arange(nk, dtype=I32)
    experts = jnp.arange(N_EXPERTS, dtype=I32)
    counts = jnp.sum((flat_e[:, None] == experts[None, :]).astype(I32), axis=0)
    first = jnp.cumsum(counts) - counts
    blocks_per_e = (counts + EXPERT_ROWS - 1) // EXPERT_ROWS
    blk_end = jnp.cumsum(blocks_per_e)
    blk_start = blk_end - blocks_per_e

    order = jnp.sort(flat_e * nk + ids)
    sorted_id = order % nk
    shift = blk_start * EXPERT_ROWS - first
    sorted_dest = ids + jnp.take(shift, order // nk)
    _, dest = lax.sort_key_val(sorted_id, sorted_dest)
    sorted_tok = jnp.concatenate([sorted_id // TOP_K, jnp.arange(EXPERT_ROWS, dtype=I32) % (nk // TOP_K)])

    b = jnp.arange(n_blocks, dtype=I32)
    used = blk_end[-1]
    src = jnp.minimum(b, used - 1)
    blk_expert = jnp.minimum(jnp.sum((blk_end[None, :] <= src[:, None]).astype(I32), axis=1), N_EXPERTS - 1)
    row0 = (src - blk_start[blk_expert]) * EXPERT_ROWS
    rows_here = jnp.clip(counts[blk_expert] - row0, 0, EXPERT_ROWS)
    grains = jnp.where(b < used, (rows_here + EXPERT_GRAIN - 1) // EXPERT_GRAIN, 0)
    blk_off = first[blk_expert] + row0
    i32 = lambda a: a.astype(I32)
    return i32(dest), i32(sorted_tok), i32(blk_expert), i32(src), i32(grains), i32(blk_off)


def kernel(x, mem, norm_mix, w_in, pool_lin, pool_scale, w_branch_a, w_branch_b, w_gate, b_gate, w_mix_out,
           xa_norm, mem_norm, xa_wq, xa_wkv, xa_wo, moe_norm, w_router, b_router, w_up, b_up, w_down, b_down,
           final_norm):
    batch, s, d = x.shape
    assert batch == 1 and norm_mix.shape[0] == 1
    h = x.reshape(s, d)
    l = 0
    bf = lambda w: w.astype(BF16)

    qkv_w = 3 * SB_WIDTH
    q_scale = jnp.concatenate([jnp.full((SB_WIDTH,), SB_HEAD_DIM ** -0.5, F32), jnp.ones((2 * SB_WIDTH,), F32)])
    qkv = _linear(h, w_in[l], cols=(0, qkv_w), gain=norm_mix[l], col_scale=q_scale, out_dtype=BF16)
    p = _linear(h, w_in[l], cols=(qkv_w, POOL_WIDTH), gain=norm_mix[l], out_dtype=F32)
    gates = _linear(h, w_gate[l], gain=norm_mix[l], bias=b_gate[l], act="sigmoid", out_dtype=BF16)
    o_a = _sb_attention(qkv, heads=SB_HEADS)
    o_b = _pool(p, bf(pool_lin[l]), pool_scale[l])
    h = _merge(o_a, o_b, gates, h, bf(w_branch_a[l]), bf(w_branch_b[l]), bf(w_mix_out[l]))

    kv = _linear(mem.reshape(mem.shape[1], d), xa_wkv[l], gain=mem_norm[l], out_dtype=BF16)
    h, top_idx, gate = _xattn_router(h, xa_norm[l], bf(xa_wq[l]), kv[:, :XA_WIDTH], kv[:, XA_WIDTH:],
                                     bf(xa_wo[l]), moe_norm[l], w_router[l], b_router[l])

    n_blocks = (s * TOP_K) // EXPERT_ROWS + N_EXPERTS
    dest, sorted_tok, blk_expert, blk_src, blk_grains, blk_off = _routing_tables(top_idx, n_blocks)
    ys = _experts(h, moe_norm[l], sorted_tok, blk_expert, blk_src, blk_grains, blk_off,
                  w_up[l], b_up[l], w_down[l], b_down[l])
    out = _combine(ys, dest, gate, h, final_norm)
    return out.reshape(batch, s, d)
```

```python
import functools

import jax
import jax.numpy as jnp
from jax import lax
from jax.experimental import pallas as pl
from jax.experimental.pallas import tpu as pltpu

F32 = jnp.float32
BF16 = jnp.bfloat16
U32 = jnp.uint32
I32 = jnp.int32

RMS_EPS = 1e-5

SB_HEADS = 8
SB_HEAD_DIM = 128
SB_WIDTH = SB_HEADS * SB_HEAD_DIM
POOL_WINDOWS = (2, 4, 8, 16)
POOL_GROUP_DIM = 256
POOL_WIDTH = len(POOL_WINDOWS) * POOL_GROUP_DIM
POOL_HALO = 16
XA_HEADS = 4
XA_HEAD_DIM = 128
XA_WIDTH = XA_HEADS * XA_HEAD_DIM
N_EXPERTS = 32
TOP_K = 4
SWIGLU_ALPHA = 1.702
SWIGLU_LIMIT = 7.0

VMEM_LIMIT_BYTES = 56 * 1024 * 1024

SB_DEAD_LOG = 110.0

EXPERT_ROWS = 1280
EXPERT_GRAIN = 128
EXPERT_FT = 256
EXPERT_DOWN_CHUNK = 512


def _params(*sem):
    return pltpu.CompilerParams(dimension_semantics=sem, vmem_limit_bytes=VMEM_LIMIT_BYTES)


def _rms(x, gain):
    return x * lax.rsqrt(jnp.mean(x * x, axis=-1, keepdims=True) + RMS_EPS) * gain


def _linear_body(*refs, rms, has_scale, has_bias, act):
    refs = list(refs)
    a_ref = refs.pop(0)
    g_ref = refs.pop(0) if rms else None
    w_ref = refs.pop(0)
    s_ref = refs.pop(0) if has_scale else None
    b_ref = refs.pop(0) if has_bias else None
    o_ref, abf_ref = refs

    @pl.when(pl.program_id(1) == 0)
    def _():
        a = a_ref[...].astype(F32)
        if rms:
            a = _rms(a, g_ref[...])
        abf_ref[...] = a.astype(BF16)

    acc = jnp.dot(abf_ref[...], w_ref[...], preferred_element_type=F32)
    if has_scale:
        acc = acc * s_ref[...]
    if has_bias:
        acc = acc + b_ref[...]
    if act == "sigmoid":
        acc = jax.nn.sigmoid(acc)
    o_ref[...] = acc.astype(o_ref.dtype)


def _linear(a, w, *, gain=None, col_scale=None, bias=None, act=None, out_dtype=F32, tm=1024, tn=1024):
    m, k = a.shape
    n = w.shape[1]
    tm, tn = min(tm, m), min(tn, n)
    assert m % tm == 0 and n % tn == 0
    args, specs = [a], [pl.BlockSpec((tm, k), lambda i, j: (i, 0))]
    if gain is not None:
        args.append(gain.reshape(1, k).astype(F32))
        specs.append(pl.BlockSpec((1, k), lambda i, j: (0, 0)))
    args.append(w)
    specs.append(pl.BlockSpec((k, tn), lambda i, j: (0, j)))
    for vec in (col_scale, bias):
        if vec is not None:
            args.append(vec.reshape(1, n).astype(F32))
            specs.append(pl.BlockSpec((1, tn), lambda i, j: (0, j)))
    body = functools.partial(_linear_body, rms=gain is not None, has_scale=col_scale is not None,
                             has_bias=bias is not None, act=act)
    return pl.pallas_call(
        body,
        grid=(m // tm, n // tn),
        in_specs=specs,
        out_specs=pl.BlockSpec((tm, tn), lambda i, j: (i, j)),
        out_shape=jax.ShapeDtypeStruct((m, n), out_dtype),
        scratch_shapes=[pltpu.VMEM((tm, k), BF16)],
        compiler_params=_params("parallel", "arbitrary"),
    )(*args)


def _sb_attn_body(q_ref, k_ref, v_ref, o_ref, acc_ref, run_ref, *, t, group):
    i = pl.program_id(1)
    dh = SB_HEAD_DIM
    row = lax.broadcasted_iota(I32, (t, t), 0)
    col = lax.broadcasted_iota(I32, (t, t), 1)
    causal = col < row
    later_keys = (row > col).astype(BF16)
    later_keys = jnp.concatenate([later_keys, later_keys], axis=0)
    heads = range(group)
    head_cols = [slice(h * dh, (h + 1) * dh) for h in heads]

    def tile(kb, diagonal):
        start = pl.multiple_of(kb * t, t)
        z = [lax.dot_general(q_ref[:, c], k_ref[pl.ds(start, t), c], (((1,), (1,)), ((), ())),
                             preferred_element_type=F32) for c in head_cols]
        softplus = [jnp.maximum(x, 0.0) + jnp.log(1.0 + jnp.exp(-jnp.abs(x))) for x in z]
        neg_log_keep = [jnp.where(causal, x, 0.0) for x in softplus] if diagonal else softplus
        later = []
        for x in neg_log_keep:
            hi = x.astype(BF16)
            lo = (x - hi.astype(F32)).astype(BF16)
            later.append(jnp.dot(jnp.concatenate([hi, lo], axis=1), later_keys, preferred_element_type=F32))
        run_min = None
        for h in heads:
            if diagonal:
                w = jnp.where(causal, jnp.exp(z[h] - softplus[h] - later[h]), 0.0)
                run = jnp.sum(neg_log_keep[h], axis=1, keepdims=True)
            else:
                run = run_ref[h]
                w = jnp.exp(z[h] - softplus[h] - later[h] - run)
                run = run + jnp.sum(neg_log_keep[h], axis=1, keepdims=True)
            pv = jnp.dot(w.astype(BF16), v_ref[pl.ds(start, t), head_cols[h]], preferred_element_type=F32)
            if diagonal:
                acc_ref[h] = pv
            else:
                acc_ref[h] += pv
            run_ref[h] = run
            head_min = jnp.min(run)
            run_min = head_min if run_min is None else jnp.minimum(run_min, head_min)
        return run_min

    def cond(carry):
        kb, run_min = carry
        return jnp.logical_and(kb >= 0, run_min < SB_DEAD_LOG)

    def step(carry):
        kb, _ = carry
        return kb - 1, tile(kb, False)

    lax.while_loop(cond, step, (i - 1, tile(i, True)))
    for h in range(group):
        o_ref[:, h * dh:(h + 1) * dh] = acc_ref[h].astype(o_ref.dtype)


def _sb_attention(qkv, *, heads, t=256, group=4):
    s = qkv.shape[0]
    dh = SB_HEAD_DIM
    t, group = min(t, s), min(group, heads)
    assert s % t == 0 and heads % group == 0
    n_groups = heads // group
    width = group * dh
    return pl.pallas_call(
        functools.partial(_sb_attn_body, t=t, group=group),
        grid=(n_groups, s // t),
        in_specs=[
            pl.BlockSpec((t, width), lambda g, i: (i, g)),
            pl.BlockSpec((s, width), lambda g, i: (0, n_groups + g)),
            pl.BlockSpec((s, width), lambda g, i: (0, 2 * n_groups + g)),
        ],
        out_specs=pl.BlockSpec((t, width), lambda g, i: (i, g)),
        out_shape=jax.ShapeDtypeStruct((s, heads * dh), BF16),
        scratch_shapes=[pltpu.VMEM((group, t, dh), F32), pltpu.VMEM((group, t, 1), F32)],
        compiler_params=_params("parallel", "arbitrary"),
    )(qkv, qkv, qkv)


def _pool_body(p_ref, halo_ref, lin_ref, scale_ref, o_ref, *, ts):
    i = pl.program_id(0)
    cur = p_ref[...]
    halo = jnp.where(i > 0, halo_ref[...], 0.0)
    ext = jnp.concatenate([halo, cur], axis=0)
    pos = i * ts + lax.broadcasted_iota(I32, (ts, 1), 0)
    c = POOL_GROUP_DIM
    for g, window in enumerate(POOL_WINDOWS):
        total = ext[:, g * c:(g + 1) * c]
        span = 1
        while span < window:
            total = total + pltpu.roll(total, span, 0)
            span *= 2
        total = total[POOL_HALO:]
        count = jnp.minimum(pos + 1, window).astype(F32)
        pooled = total / count - cur[:, g * c:(g + 1) * c]
        mixed = jnp.dot(pooled.astype(BF16), lin_ref[g], preferred_element_type=F32)
        o_ref[:, g * c:(g + 1) * c] = (mixed * scale_ref[:, g * c:(g + 1) * c]).astype(o_ref.dtype)


def _pool(p, pool_lin, pool_scale, *, ts=512):
    s, width = p.shape
    ts = min(ts, s)
    assert s % ts == 0 and ts % POOL_HALO == 0
    per = ts // POOL_HALO
    groups = len(POOL_WINDOWS)
    return pl.pallas_call(
        functools.partial(_pool_body, ts=ts),
        grid=(s // ts,),
        in_specs=[
            pl.BlockSpec((ts, width), lambda i: (i, 0)),
            pl.BlockSpec((POOL_HALO, width), lambda i: (jnp.maximum(i * per - 1, 0), 0)),
            pl.BlockSpec((groups, POOL_GROUP_DIM, POOL_GROUP_DIM), lambda i: (0, 0, 0)),
            pl.BlockSpec((1, width), lambda i: (0, 0)),
        ],
        out_specs=pl.BlockSpec((ts, width), lambda i: (i, 0)),
        out_shape=jax.ShapeDtypeStruct((s, width), BF16),
        compiler_params=_params("parallel"),
    )(p, p, pool_lin, pool_scale.reshape(1, width).astype(F32))


def _merge_body(oa_ref, ob_ref, ga_ref, gb_ref, h_ref, wa_ref, wb_ref, wm_ref, o_ref):
    ya = jnp.dot(oa_ref[...], wa_ref[...], preferred_element_type=F32)
    yb = jnp.dot(ob_ref[...], wb_ref[...], preferred_element_type=F32)
    merged = ga_ref[...].astype(F32) * ya + gb_ref[...].astype(F32) * yb
    o_ref[...] = h_ref[...] + jnp.dot(merged.astype(BF16), wm_ref[...], preferred_element_type=F32)


def _const_spec(shape):
    return pl.BlockSpec(shape, lambda i: (0,) * len(shape), pipeline_mode=pl.Buffered(1))


def _merge(o_a, o_b, gates, h, w_a, w_b, w_mix, *, tm=256):
    s, d = h.shape
    tm = min(tm, s)
    return pl.pallas_call(
        _merge_body,
        grid=(s // tm,),
        in_specs=[
            pl.BlockSpec((tm, o_a.shape[1]), lambda i: (i, 0)),
            pl.BlockSpec((tm, o_b.shape[1]), lambda i: (i, 0)),
            pl.BlockSpec((tm, d), lambda i: (i, 0)),
            pl.BlockSpec((tm, d), lambda i: (i, 1)),
            pl.BlockSpec((tm, d), lambda i: (i, 0)),
            _const_spec(w_a.shape), _const_spec(w_b.shape), _const_spec(w_mix.shape),
        ],
        out_specs=pl.BlockSpec((tm, d), lambda i: (i, 0)),
        out_shape=jax.ShapeDtypeStruct((s, d), F32),
        compiler_params=_params("parallel"),
    )(o_a, o_b, gates, gates, h, w_a, w_b, w_mix)


def _xattn_router_body(h_ref, xg_ref, wq_ref, k_ref, v_ref, wo_ref, mg_ref, wr_ref, br_ref,
                       h2_ref, idx_ref, gate_ref):
    h = h_ref[...]
    u = _rms(h, xg_ref[...]).astype(BF16)
    q = (jnp.dot(u, wq_ref[...], preferred_element_type=F32) * (XA_HEAD_DIM ** -0.5)).astype(BF16)
    heads = []
    for hd in range(XA_HEADS):
        sl = slice(hd * XA_HEAD_DIM, (hd + 1) * XA_HEAD_DIM)
        s = lax.dot_general(q[:, sl], k_ref[:, sl], (((1,), (1,)), ((), ())), preferred_element_type=F32)
        p = jnp.exp(s - jnp.max(s, axis=-1, keepdims=True))
        o = jnp.dot(p.astype(BF16), v_ref[:, sl], preferred_element_type=F32)
        heads.append(o / jnp.sum(p, axis=-1, keepdims=True))
    o = jnp.concatenate(heads, axis=1).astype(BF16)
    h2 = h + jnp.dot(o, wo_ref[...], preferred_element_type=F32)
    h2_ref[...] = h2

    t = _rms(h2, mg_ref[...])
    t_hi = t.astype(BF16)
    t_lo = (t - t_hi.astype(F32)).astype(BF16)
    n_e = br_ref.shape[1]
    by_hi = jnp.dot(t_hi, wr_ref[...], preferred_element_type=F32)
    by_lo = jnp.dot(t_lo, wr_ref[:, :n_e], preferred_element_type=F32)
    logits = by_hi[:, :n_e] + by_hi[:, n_e:] + by_lo + br_ref[...]
    lane = lax.broadcasted_iota(I32, logits.shape, 1).astype(F32)
    vals, idxs = [], []
    for _ in range(TOP_K):
        best = jnp.max(logits, axis=-1, keepdims=True)
        arg = jnp.min(jnp.where(logits == best, lane, float(n_e)), axis=-1, keepdims=True)
        vals.append(best)
        idxs.append(arg)
        logits = jnp.where(lane == arg, -jnp.inf, logits)
    e = jnp.exp(jnp.concatenate(vals, axis=1) - vals[0])
    gate_ref[...] = e / jnp.sum(e, axis=-1, keepdims=True)
    idx_ref[...] = jnp.concatenate(idxs, axis=1).astype(I32)


def _xattn_router(h, xa_gain, wq, kx, vx, wo, moe_gain, w_router, b_router, *, tm=512):
    s, d = h.shape
    tm = min(tm, s)
    n_e = w_router.shape[1]
    w_hi = w_router.astype(BF16)
    w_router = jnp.concatenate([w_hi, (w_router - w_hi.astype(F32)).astype(BF16)], axis=1)
    row = lambda width: pl.BlockSpec((tm, width), lambda i: (i, 0))
    return pl.pallas_call(
        _xattn_router_body,
        grid=(s // tm,),
        in_specs=[
            row(d), _const_spec((1, d)), _const_spec(wq.shape), _const_spec(kx.shape), _const_spec(vx.shape),
            _const_spec(wo.shape), _const_spec((1, d)), _const_spec(w_router.shape), _const_spec((1, n_e)),
        ],
        out_specs=[row(d), row(TOP_K), row(TOP_K)],
        out_shape=[
            jax.ShapeDtypeStruct((s, d), F32),
            jax.ShapeDtypeStruct((s, TOP_K), I32),
            jax.ShapeDtypeStruct((s, TOP_K), F32),
        ],
        compiler_params=_params("parallel"),
    )(h, xa_gain.reshape(1, d), wq, kx, vx, wo, moe_gain.reshape(1, d), w_router, b_router.reshape(1, n_e))


def _expert_body(blk_e_ref, blk_src_ref, grains_ref, off_ref, tok_ref,
                 h_hbm, gain_ref, wup_hbm, wdn_hbm, bup_ref, bdn_ref, o_ref,
                 xst_ref, xb_ref, wg_st, wl_st, wd_st, wgb_ref, wlb_ref, wdb_ref, wsem, xsem, *, nj, ft, n_blocks):
    del blk_src_ref
    b = pl.program_id(0)
    grains = grains_ref[b]
    nxt = jnp.minimum(b + 1, n_blocks - 1)
    has_next = jnp.logical_and(b + 1 < n_blocks, grains_ref[nxt] > 0)
    d = o_ref.shape[1]
    n_grains = EXPERT_ROWS // EXPERT_GRAIN

    def weight_copies(blk, j, slot):
        e = blk_e_ref[blk]
        c = pl.multiple_of(j * ft, ft)
        return (
            pltpu.make_async_copy(wup_hbm.at[e, :, pl.ds(c, ft)], wg_st.at[slot], wsem.at[slot, 0]),
            pltpu.make_async_copy(wup_hbm.at[e, :, pl.ds(nj * ft + c, ft)], wl_st.at[slot], wsem.at[slot, 1]),
            pltpu.make_async_copy(wdn_hbm.at[e, pl.ds(c, ft), :], wd_st.at[slot], wsem.at[slot, 2]),
        )

    def row_copy(blk, q, r):
        row = q * EXPERT_GRAIN + r
        return pltpu.make_async_copy(h_hbm.at[pl.ds(tok_ref[off_ref[blk] + row], 1), :],
                                     xst_ref.at[pl.ds(row, 1), :], xsem.at[q])

    @pl.when(b == 0)
    def _():
        for copy in weight_copies(0, 0, 0):
            copy.start()

        def start(i, carry):
            row_copy(0, i // EXPERT_GRAIN, i % EXPERT_GRAIN).start()
            return carry

        lax.fori_loop(0, grains * EXPERT_GRAIN, start, 0)

    @pl.when(grains > 0)
    def _():
        for q in range(n_grains):
            @pl.when(q < grains)
            def _():
                def drain(r, carry):
                    row_copy(b, q, r).wait()
                    return carry

                lax.fori_loop(0, EXPERT_GRAIN, drain, 0, unroll=8)
                grain = pl.ds(q * EXPERT_GRAIN, EXPERT_GRAIN)
                xb_ref[grain, :] = _rms(xst_ref[grain, :], gain_ref[...]).astype(BF16)

        o_ref[...] = jnp.broadcast_to(bdn_ref[0], o_ref.shape)

        def hidden_tile(j, carry):
            slot = j % 2
            for copy in weight_copies(b, j, slot):
                copy.wait()

            @pl.when(j + 1 < nj)
            def _():
                for copy in weight_copies(b, j + 1, 1 - slot):
                    copy.start()

            @pl.when(jnp.logical_and(j + 1 == nj, has_next))
            def _():
                for copy in weight_copies(nxt, 0, 1 - slot):
                    copy.start()

            wgb_ref[...] = wg_st[slot].astype(BF16)
            wlb_ref[...] = wl_st[slot].astype(BF16)
            wdb_ref[...] = wd_st[slot].astype(BF16)

            for q in range(n_grains):
                @pl.when(jnp.logical_and(jnp.logical_and(has_next, q < grains_ref[nxt]), j == q % nj))
                def _():
                    for r in range(EXPERT_GRAIN):
                        row_copy(nxt, q, r).start()

            bias_g = bup_ref[0, pl.ds(j, 1), :]
            bias_l = bup_ref[0, pl.ds(nj + j, 1), :]
            for m_grains in range(1, n_grains + 1):
                @pl.when(grains == m_grains)
                def _():
                    m = m_grains * EXPERT_GRAIN
                    x = xb_ref[0:m, :]
                    hg = jnp.dot(x, wgb_ref[...], preferred_element_type=F32) + bias_g
                    hl = jnp.dot(x, wlb_ref[...], preferred_element_type=F32) + bias_l
                    g = jnp.minimum(hg, SWIGLU_LIMIT)
                    l = jnp.clip(hl, -SWIGLU_LIMIT, SWIGLU_LIMIT)
                    act = (g * jax.nn.sigmoid(SWIGLU_ALPHA * g) * (l + 1.0)).astype(BF16)
                    for n in range(0, d, EXPERT_DOWN_CHUNK):
                        cols = slice(n, min(n + EXPERT_DOWN_CHUNK, d))
                        o_ref[0:m, cols] += jnp.dot(act, wdb_ref[:, cols], preferred_element_type=F32)
            return carry

        lax.fori_loop(0, nj, hidden_tile, 0)


def _experts(h, gain, sorted_tok, blk_expert, blk_src, blk_grains, blk_off, w_up, b_up, w_down, b_down):
    d = h.shape[1]
    n_e, _, two_f = w_up.shape
    f = two_f // 2
    ft = min(EXPERT_FT, f)
    nj = f // ft
    assert nj % 2 == 0
    nb = blk_expert.shape[0]
    n_grains = EXPERT_ROWS // EXPERT_GRAIN
    by_expert = lambda b, be, bs, gr, off, tok: (be[b], 0, 0)
    return pl.pallas_call(
        functools.partial(_expert_body, nj=nj, ft=ft, n_blocks=nb),
        grid_spec=pltpu.PrefetchScalarGridSpec(
            num_scalar_prefetch=5,
            grid=(nb,),
            in_specs=[
                pl.BlockSpec(memory_space=pl.ANY),
                pl.BlockSpec((1, d), lambda b, *_: (0, 0)),
                pl.BlockSpec(memory_space=pl.ANY),
                pl.BlockSpec(memory_space=pl.ANY),
                pl.BlockSpec((1, 2 * nj, ft), by_expert),
                pl.BlockSpec((1, 1, d), by_expert),
            ],
            out_specs=pl.BlockSpec((EXPERT_ROWS, d), lambda b, be, bs, gr, off, tok: (bs[b], 0)),
            scratch_shapes=[
                pltpu.VMEM((EXPERT_ROWS, d), F32), pltpu.VMEM((EXPERT_ROWS, d), BF16),
                pltpu.VMEM((2, d, ft), F32), pltpu.VMEM((2, d, ft), F32), pltpu.VMEM((2, ft, d), F32),
                pltpu.VMEM((d, ft), BF16), pltpu.VMEM((d, ft), BF16), pltpu.VMEM((ft, d), BF16),
                pltpu.SemaphoreType.DMA((2, 3)), pltpu.SemaphoreType.DMA((n_grains,)),
            ],
        ),
        out_shape=jax.ShapeDtypeStruct((nb * EXPERT_ROWS, d), F32),
        compiler_params=_params("arbitrary"),
    )(blk_expert, blk_src, blk_grains, blk_off, sorted_tok,
      h, gain.reshape(1, d), w_up, w_down, b_up.reshape(n_e, 2 * nj, ft), b_down.reshape(n_e, 1, d))


def _combine_body(dest_ref, ys_hbm, gate_ref, h_ref, fg_ref, o_ref, buf_ref, sem, *, tt, n_tiles):
    i = pl.program_id(0)

    def row_copy(tile, slot, t, k):
        return pltpu.make_async_copy(ys_hbm.at[pl.ds(dest_ref[(tile * tt + t) * TOP_K + k], 1), :],
                                     buf_ref.at[slot, k, pl.ds(t, 1), :], sem.at[slot])

    for slot in range(2):
        @pl.when(jnp.logical_and(i < n_tiles, i % 2 == slot))
        def _():
            for t in range(tt):
                for k in range(TOP_K):
                    row_copy(i, slot, t, k).start()

    @pl.when(i >= 1)
    def _():
        slot = (i - 1) % 2

        def drain(t, carry):
            for k in range(TOP_K):
                row_copy(i - 1, slot, t, k).wait()
            return carry

        lax.fori_loop(0, tt, drain, 0, unroll=4)
        gate = gate_ref[...]
        y = h_ref[...]
        for k in range(TOP_K):
            y = y + gate[:, k:k + 1] * buf_ref[slot, k]
        o_ref[...] = _rms(y, fg_ref[...])


def _combine(ys, dest, gate, h, final_gain, *, tt=128):
    n, d = h.shape
    tt = min(tt, n)
    n_tiles = n // tt
    prev = lambda i, dst: (jnp.maximum(i - 1, 0), 0)
    return pl.pallas_call(
        functools.partial(_combine_body, tt=tt, n_tiles=n_tiles),
        grid_spec=pltpu.PrefetchScalarGridSpec(
            num_scalar_prefetch=1,
            grid=(n_tiles + 1,),
            in_specs=[
                pl.BlockSpec(memory_space=pl.ANY),
                pl.BlockSpec((tt, TOP_K), prev),
                pl.BlockSpec((tt, d), prev),
                pl.BlockSpec((1, d), lambda i, dst: (0, 0)),
            ],
            out_specs=pl.BlockSpec((tt, d), prev),
            scratch_shapes=[pltpu.VMEM((2, TOP_K, tt, d), F32), pltpu.SemaphoreType.DMA((2,))],
        ),
        out_shape=jax.ShapeDtypeStruct((n, d), F32),
        compiler_params=_params("arbitrary"),
    )(dest, ys, gate, h, final_gain.reshape(1, d))


def _routing_tables(top_idx, n_blocks):
    flat_e = top_idx.reshape(-1)
    nk = flat_e.shape[0]
    ids = jnp.arange(nk, dtype=I32)
    experts = jnp.arange(N_EXPERTS, dtype=I32)
    counts = jnp.sum((flat_e[:, None] == experts[None, :]).astype(I32), axis=0)
    first = jnp.cumsum(counts) - counts
    blocks_per_e = (counts + EXPERT_ROWS - 1) // EXPERT_ROWS
    blk_end = jnp.cumsum(blocks_per_e)
    blk_start = blk_end - blocks_per_e

    order = jnp.sort(flat_e * nk + ids)
    sorted_id = order % nk
    shift = blk_start * EXPERT_ROWS - first
    sorted_dest = ids + jnp.take(shift, order // nk)
    _, dest = lax.sort_key_val(sorted_id, sorted_dest)
    sorted_tok = jnp.concatenate([sorted_id // TOP_K, jnp.arange(EXPERT_ROWS, dtype=I32) % (nk // TOP_K)])

    b = jnp.arange(n_blocks, dtype=I32)
    used = blk_end[-1]
    src = jnp.minimum(b, used - 1)
    blk_expert = jnp.minimum(jnp.sum((blk_end[None, :] <= src[:, None]).astype(I32), axis=1), N_EXPERTS - 1)
    row0 = (src - blk_start[blk_expert]) * EXPERT_ROWS
    rows_here = jnp.clip(counts[blk_expert] - row0, 0, EXPERT_ROWS)
    grains = jnp.where(b < used, (rows_here + EXPERT_GRAIN - 1) // EXPERT_GRAIN, 0)
    blk_off = first[blk_expert] + row0
    i32 = lambda a: a.astype(I32)
    return i32(dest), i32(sorted_tok), i32(blk_expert), i32(src), i32(grains), i32(blk_off)


def kernel(x, mem, norm_mix, w_in, pool_lin, pool_scale, w_branch_a, w_branch_b, w_gate, b_gate, w_mix_out,
           xa_norm, mem_norm, xa_wq, xa_wkv, xa_wo, moe_norm, w_router, b_router, w_up, b_up, w_down, b_down,
           final_norm):
    batch, s, d = x.shape
    assert batch == 1 and norm_mix.shape[0] == 1
    h = x.reshape(s, d)
    l = 0
    bf = lambda w: w.astype(BF16)

    qkv_w = 3 * SB_WIDTH
    q_scale = jnp.concatenate([jnp.full((SB_WIDTH,), SB_HEAD_DIM ** -0.5, F32), jnp.ones((2 * SB_WIDTH,), F32)])
    qkv = _linear(h, bf(w_in[l][:, :qkv_w]), gain=norm_mix[l], col_scale=q_scale, out_dtype=BF16)
    p = _linear(h, bf(w_in[l][:, qkv_w:]), gain=norm_mix[l], out_dtype=F32)
    gates = _linear(h, bf(w_gate[l]), gain=norm_mix[l], bias=b_gate[l], act="sigmoid", out_dtype=BF16)
    o_a = _sb_attention(qkv, heads=SB_HEADS)
    o_b = _pool(p, bf(pool_lin[l]), pool_scale[l])
    h = _merge(o_a, o_b, gates, h, bf(w_branch_a[l]), bf(w_branch_b[l]), bf(w_mix_out[l]))

    kv = _linear(mem.reshape(mem.shape[1], d), bf(xa_wkv[l]), gain=mem_norm[l], out_dtype=BF16)
    h, top_idx, gate = _xattn_router(h, xa_norm[l], bf(xa_wq[l]), kv[:, :XA_WIDTH], kv[:, XA_WIDTH:],
                                     bf(xa_wo[l]), moe_norm[l], w_router[l], b_router[l])

    n_blocks = (s * TOP_K) // EXPERT_ROWS + N_EXPERTS
    dest, sorted_tok, blk_expert, blk_src, blk_grains, blk_off = _routing_tables(top_idx, n_blocks)
    ys = _experts(h, moe_norm[l], sorted_tok, blk_expert, blk_src, blk_grains, blk_off,
                  w_up[l], b_up[l], w_down[l], b_down[l])
    out = _combine(ys, dest, gate, h, final_norm)
    return out.reshape(batch, s, d)
```

```python
import functools

import jax
import jax.numpy as jnp
from jax import lax
from jax.experimental import pallas as pl
from jax.experimental.pallas import tpu as pltpu

F32 = jnp.float32
BF16 = jnp.bfloat16
U32 = jnp.uint32
I32 = jnp.int32

RMS_EPS = 1e-5

SB_HEADS = 8
SB_HEAD_DIM = 128
SB_WIDTH = SB_HEADS * SB_HEAD_DIM
POOL_WINDOWS = (2, 4, 8, 16)
POOL_GROUP_DIM = 256
POOL_WIDTH = len(POOL_WINDOWS) * POOL_GROUP_DIM
POOL_HALO = 16
XA_HEADS = 4
XA_HEAD_DIM = 128
XA_WIDTH = XA_HEADS * XA_HEAD_DIM
N_EXPERTS = 32
TOP_K = 4
SWIGLU_ALPHA = 1.702
SWIGLU_LIMIT = 7.0

VMEM_LIMIT_BYTES = 56 * 1024 * 1024

SB_DEAD_LOG = 110.0

EXPERT_ROWS = 1280
EXPERT_GRAIN = 128
EXPERT_FT = 256
EXPERT_DOWN_CHUNK = 512


def _params(*sem):
    return pltpu.CompilerParams(dimension_semantics=sem, vmem_limit_bytes=VMEM_LIMIT_BYTES)


def _rms(x, gain):
    return x * lax.rsqrt(jnp.mean(x * x, axis=-1, keepdims=True) + RMS_EPS) * gain


def _linear_body(*refs, rms, has_scale, has_bias, act):
    refs = list(refs)
    a_ref = refs.pop(0)
    g_ref = refs.pop(0) if rms else None
    w_ref = refs.pop(0)
    s_ref = refs.pop(0) if has_scale else None
    b_ref = refs.pop(0) if has_bias else None
    o_ref, abf_ref = refs

    @pl.when(pl.program_id(1) == 0)
    def _():
        a = a_ref[...].astype(F32)
        if rms:
            a = _rms(a, g_ref[...])
        abf_ref[...] = a.astype(BF16)

    acc = jnp.dot(abf_ref[...], w_ref[...], preferred_element_type=F32)
    if has_scale:
        acc = acc * s_ref[...]
    if has_bias:
        acc = acc + b_ref[...]
    if act == "sigmoid":
        acc = jax.nn.sigmoid(acc)
    o_ref[...] = acc.astype(o_ref.dtype)


def _linear(a, w, *, gain=None, col_scale=None, bias=None, act=None, out_dtype=F32, tm=1024, tn=1024):
    m, k = a.shape
    n = w.shape[1]
    tm, tn = min(tm, m), min(tn, n)
    assert m % tm == 0 and n % tn == 0
    args, specs = [a], [pl.BlockSpec((tm, k), lambda i, j: (i, 0))]
    if gain is not None:
        args.append(gain.reshape(1, k).astype(F32))
        specs.append(pl.BlockSpec((1, k), lambda i, j: (0, 0)))
    args.append(w)
    specs.append(pl.BlockSpec((k, tn), lambda i, j: (0, j)))
    for vec in (col_scale, bias):
        if vec is not None:
            args.append(vec.reshape(1, n).astype(F32))
            specs.append(pl.BlockSpec((1, tn), lambda i, j: (0, j)))
    body = functools.partial(_linear_body, rms=gain is not None, has_scale=col_scale is not None,
                             has_bias=bias is not None, act=act)
    return pl.pallas_call(
        body,
        grid=(m // tm, n // tn),
        in_specs=specs,
        out_specs=pl.BlockSpec((tm, tn), lambda i, j: (i, j)),
        out_shape=jax.ShapeDtypeStruct((m, n), out_dtype),
        scratch_shapes=[pltpu.VMEM((tm, k), BF16)],
        compiler_params=_params("parallel", "arbitrary"),
    )(*args)


def _sb_attn_body(q_ref, k_ref, v_ref, o_ref, acc_ref, run_ref, *, t, group):
    i = pl.program_id(1)
    dh = SB_HEAD_DIM
    row = lax.broadcasted_iota(I32, (t, t), 0)
    col = lax.broadcasted_iota(I32, (t, t), 1)
    causal = col < row
    later_keys = (row > col).astype(BF16)
    later_keys = jnp.concatenate([later_keys, later_keys], axis=0)
    heads = range(group)
    head_cols = [slice(h * dh, (h + 1) * dh) for h in heads]

    def tile(kb, diagonal):
        start = pl.multiple_of(kb * t, t)
        z = [lax.dot_general(q_ref[:, c], k_ref[pl.ds(start, t), c], (((1,), (1,)), ((), ())),
                             preferred_element_type=F32) for c in head_cols]
        softplus = [jnp.maximum(x, 0.0) + jnp.log(1.0 + jnp.exp(-jnp.abs(x))) for x in z]
        neg_log_keep = [jnp.where(causal, x, 0.0) for x in softplus] if diagonal else softplus
        later = []
        for x in neg_log_keep:
            hi = x.astype(BF16)
            lo = (x - hi.astype(F32)).astype(BF16)
            later.append(jnp.dot(jnp.concatenate([hi, lo], axis=1), later_keys, preferred_element_type=F32))
        run_min = None
        for h in heads:
            if diagonal:
                w = jnp.where(causal, jnp.exp(z[h] - softplus[h] - later[h]), 0.0)
                run = jnp.sum(neg_log_keep[h], axis=1, keepdims=True)
            else:
                run = run_ref[h]
                w = jnp.exp(z[h] - softplus[h] - later[h] - run)
                run = run + jnp.sum(neg_log_keep[h], axis=1, keepdims=True)
            pv = jnp.dot(w.astype(BF16), v_ref[pl.ds(start, t), head_cols[h]], preferred_element_type=F32)
            if diagonal:
                acc_ref[h] = pv
            else:
                acc_ref[h] += pv
            run_ref[h] = run
            head_min = jnp.min(run)
            run_min = head_min if run_min is None else jnp.minimum(run_min, head_min)
        return run_min

    def cond(carry):
        kb, run_min = carry
        return jnp.logical_and(kb >= 0, run_min < SB_DEAD_LOG)

    def step(carry):
        kb, _ = carry
        return kb - 1, tile(kb, False)

    lax.while_loop(cond, step, (i - 1, tile(i, True)))
    for h in range(group):
        o_ref[:, h * dh:(h + 1) * dh] = acc_ref[h].astype(o_ref.dtype)


def _sb_attention(qkv, *, heads, t=256, group=4):
    s = qkv.shape[0]
    dh = SB_HEAD_DIM
    t, group = min(t, s), min(group, heads)
    assert s % t == 0 and heads % group == 0
    n_groups = heads // group
    width = group * dh
    return pl.pallas_call(
        functools.partial(_sb_attn_body, t=t, group=group),
        grid=(n_groups, s // t),
        in_specs=[
            pl.BlockSpec((t, width), lambda g, i: (i, g)),
            pl.BlockSpec((s, width), lambda g, i: (0, n_groups + g)),
            pl.BlockSpec((s, width), lambda g, i: (0, 2 * n_groups + g)),
        ],
        out_specs=pl.BlockSpec((t, width), lambda g, i: (i, g)),
        out_shape=jax.ShapeDtypeStruct((s, heads * dh), BF16),
        scratch_shapes=[pltpu.VMEM((group, t, dh), F32), pltpu.VMEM((group, t, 1), F32)],
        compiler_params=_params("parallel", "arbitrary"),
    )(qkv, qkv, qkv)


def _pool_body(p_ref, halo_ref, lin_ref, scale_ref, o_ref, *, ts):
    i = pl.program_id(0)
    cur = p_ref[...]
    halo = jnp.where(i > 0, halo_ref[...], 0.0)
    ext = jnp.concatenate([halo, cur], axis=0)
    pos = i * ts + lax.broadcasted_iota(I32, (ts, 1), 0)
    c = POOL_GROUP_DIM
    for g, window in enumerate(POOL_WINDOWS):
        total = ext[:, g * c:(g + 1) * c]
        span = 1
        while span < window:
            total = total + pltpu.roll(total, span, 0)
            span *= 2
        total = total[POOL_HALO:]
        count = jnp.minimum(pos + 1, window).astype(F32)
        pooled = total / count - cur[:, g * c:(g + 1) * c]
        mixed = jnp.dot(pooled.astype(BF16), lin_ref[g], preferred_element_type=F32)
        o_ref[:, g * c:(g + 1) * c] = (mixed * scale_ref[:, g * c:(g + 1) * c]).astype(o_ref.dtype)


def _pool(p, pool_lin, pool_scale, *, ts=512):
    s, width = p.shape
    ts = min(ts, s)
    assert s % ts == 0 and ts % POOL_HALO == 0
    per = ts // POOL_HALO
    groups = len(POOL_WINDOWS)
    return pl.pallas_call(
        functools.partial(_pool_body, ts=ts),
        grid=(s // ts,),
        in_specs=[
            pl.BlockSpec((ts, width), lambda i: (i, 0)),
            pl.BlockSpec((POOL_HALO, width), lambda i: (jnp.maximum(i * per - 1, 0), 0)),
            pl.BlockSpec((groups, POOL_GROUP_DIM, POOL_GROUP_DIM), lambda i: (0, 0, 0)),
            pl.BlockSpec((1, width), lambda i: (0, 0)),
        ],
        out_specs=pl.BlockSpec((ts, width), lambda i: (i, 0)),
        out_shape=jax.ShapeDtypeStruct((s, width), BF16),
        compiler_params=_params("parallel"),
    )(p, p, pool_lin, pool_scale.reshape(1, width).astype(F32))


def _merge_body(oa_ref, ob_ref, ga_ref, gb_ref, h_ref, wa_ref, wb_ref, wm_ref, o_ref):
    ya = jnp.dot(oa_ref[...], wa_ref[...], preferred_element_type=F32)
    yb = jnp.dot(ob_ref[...], wb_ref[...], preferred_element_type=F32)
    merged = ga_ref[...].astype(F32) * ya + gb_ref[...].astype(F32) * yb
    o_ref[...] = h_ref[...] + jnp.dot(merged.astype(BF16), wm_ref[...], preferred_element_type=F32)


def _const_spec(shape):
    return pl.BlockSpec(shape, lambda i: (0,) * len(shape), pipeline_mode=pl.Buffered(1))


def _merge(o_a, o_b, gates, h, w_a, w_b, w_mix, *, tm=256):
    s, d = h.shape
    tm = min(tm, s)
    return pl.pallas_call(
        _merge_body,
        grid=(s // tm,),
        in_specs=[
            pl.BlockSpec((tm, o_a.shape[1]), lambda i: (i, 0)),
            pl.BlockSpec((tm, o_b.shape[1]), lambda i: (i, 0)),
            pl.BlockSpec((tm, d), lambda i: (i, 0)),
            pl.BlockSpec((tm, d), lambda i: (i, 1)),
            pl.BlockSpec((tm, d), lambda i: (i, 0)),
            _const_spec(w_a.shape), _const_spec(w_b.shape), _const_spec(w_mix.shape),
        ],
        out_specs=pl.BlockSpec((tm, d), lambda i: (i, 0)),
        out_shape=jax.ShapeDtypeStruct((s, d), F32),
        compiler_params=_params("parallel"),
    )(o_a, o_b, gates, gates, h, w_a, w_b, w_mix)


def _xattn_router_body(h_ref, xg_ref, wq_ref, k_ref, v_ref, wo_ref, mg_ref, wr_ref, br_ref,
                       h2_ref, idx_ref, gate_ref):
    h = h_ref[...]
    u = _rms(h, xg_ref[...]).astype(BF16)
    q = (jnp.dot(u, wq_ref[...], preferred_element_type=F32) * (XA_HEAD_DIM ** -0.5)).astype(BF16)
    heads = []
    for hd in range(XA_HEADS):
        sl = slice(hd * XA_HEAD_DIM, (hd + 1) * XA_HEAD_DIM)
        s = lax.dot_general(q[:, sl], k_ref[:, sl], (((1,), (1,)), ((), ())), preferred_element_type=F32)
        p = jnp.exp(s - jnp.max(s, axis=-1, keepdims=True))
        o = jnp.dot(p.astype(BF16), v_ref[:, sl], preferred_element_type=F32)
        heads.append(o / jnp.sum(p, axis=-1, keepdims=True))
    o = jnp.concatenate(heads, axis=1).astype(BF16)
    h2 = h + jnp.dot(o, wo_ref[...], preferred_element_type=F32)
    h2_ref[...] = h2

    t = _rms(h2, mg_ref[...])
    t_hi = t.astype(BF16)
    t_lo = (t - t_hi.astype(F32)).astype(BF16)
    n_e = br_ref.shape[1]
    by_hi = jnp.dot(t_hi, wr_ref[...], preferred_element_type=F32)
    by_lo = jnp.dot(t_lo, wr_ref[:, :n_e], preferred_element_type=F32)
    logits = by_hi[:, :n_e] + by_hi[:, n_e:] + by_lo + br_ref[...]
    lane = lax.broadcasted_iota(I32, logits.shape, 1).astype(F32)
    vals, idxs = [], []
    for _ in range(TOP_K):
        best = jnp.max(logits, axis=-1, keepdims=True)
        arg = jnp.min(jnp.where(logits == best, lane, float(n_e)), axis=-1, keepdims=True)
        vals.append(best)
        idxs.append(arg)
        logits = jnp.where(lane == arg, -jnp.inf, logits)
    e = jnp.exp(jnp.concatenate(vals, axis=1) - vals[0])
    gate_ref[...] = e / jnp.sum(e, axis=-1, keepdims=True)
    idx_ref[...] = jnp.concatenate(idxs, axis=1).astype(I32)


def _xattn_router(h, xa_gain, wq, kx, vx, wo, moe_gain, w_router, b_router, *, tm=512):
    s, d = h.shape
    tm = min(tm, s)
    n_e = w_router.shape[1]
    w_hi = w_router.astype(BF16)
    w_router = jnp.concatenate([w_hi, (w_router - w_hi.astype(F32)).astype(BF16)], axis=1)
    row = lambda width: pl.BlockSpec((tm, width), lambda i: (i, 0))
    return pl.pallas_call(
        _xattn_router_body,
        grid=(s // tm,),
        in_specs=[
            row(d), _const_spec((1, d)), _const_spec(wq.shape), _const_spec(kx.shape), _const_spec(vx.shape),
            _const_spec(wo.shape), _const_spec((1, d)), _const_spec(w_router.shape), _const_spec((1, n_e)),
        ],
        out_specs=[row(d), row(TOP_K), row(TOP_K)],
        out_shape=[
            jax.ShapeDtypeStruct((s, d), F32),
            jax.ShapeDtypeStruct((s, TOP_K), I32),
            jax.ShapeDtypeStruct((s, TOP_K), F32),
        ],
        compiler_params=_params("parallel"),
    )(h, xa_gain.reshape(1, d), wq, kx, vx, wo, moe_gain.reshape(1, d), w_router, b_router.reshape(1, n_e))


def _expert_body(blk_e_ref, blk_src_ref, grains_ref, off_ref, tok_ref,
                 h_hbm, gain_ref, wup_hbm, wdn_hbm, bup_ref, bdn_ref, o_ref,
                 xst_ref, xb_ref, wg_st, wl_st, wd_st, wgb_ref, wlb_ref, wdb_ref, wsem, xsem, *, nj, ft, n_blocks):
    del blk_src_ref
    b = pl.program_id(0)
    grains = grains_ref[b]
    nxt = jnp.minimum(b + 1, n_blocks - 1)
    has_next = jnp.logical_and(b + 1 < n_blocks, grains_ref[nxt] > 0)
    d = o_ref.shape[1]
    n_grains = EXPERT_ROWS // EXPERT_GRAIN
    chunk = EXPERT_ROWS // nj

    def weight_copies(blk, j, slot):
        e = blk_e_ref[blk]
        c = pl.multiple_of(j * ft, ft)
        return (
            pltpu.make_async_copy(wup_hbm.at[e, :, pl.ds(c, ft)], wg_st.at[slot], wsem.at[slot, 0]),
            pltpu.make_async_copy(wup_hbm.at[e, :, pl.ds(nj * ft + c, ft)], wl_st.at[slot], wsem.at[slot, 1]),
            pltpu.make_async_copy(wdn_hbm.at[e, pl.ds(c, ft), :], wd_st.at[slot], wsem.at[slot, 2]),
        )

    def row_copy(blk, c, r):
        row = c * chunk + r
        return pltpu.make_async_copy(h_hbm.at[pl.ds(tok_ref[off_ref[blk] + row], 1), :],
                                     xst_ref.at[pl.ds(row, 1), :], xsem.at[c])

    def drain_rows(blk):
        for c in range(nj):
            def drain(r, carry, c=c):
                row_copy(blk, c, r).wait()
                return carry

            lax.fori_loop(0, chunk, drain, 0, unroll=8)

    @pl.when(b == 0)
    def _():
        for copy in weight_copies(0, 0, 0):
            copy.start()

        def start(i, carry):
            row_copy(0, i // chunk, i % chunk).start()
            return carry

        lax.fori_loop(0, EXPERT_ROWS, start, 0, unroll=8)

    @pl.when(jnp.logical_and(grains == 0, grains_ref[jnp.maximum(b - 1, 0)] > 0))
    def _():
        drain_rows(b)

    @pl.when(grains > 0)
    def _():
        drain_rows(b)
        for q in range(n_grains):
            @pl.when(q < grains)
            def _():
                grain = pl.ds(q * EXPERT_GRAIN, EXPERT_GRAIN)
                xb_ref[grain, :] = _rms(xst_ref[grain, :], gain_ref[...]).astype(BF16)

        o_ref[...] = jnp.broadcast_to(bdn_ref[0], o_ref.shape)

        def hidden_tile(j, carry):
            slot = j % 2
            for copy in weight_copies(b, j, slot):
                copy.wait()

            @pl.when(j + 1 < nj)
            def _():
                for copy in weight_copies(b, j + 1, 1 - slot):
                    copy.start()

            @pl.when(jnp.logical_and(j + 1 == nj, has_next))
            def _():
                for copy in weight_copies(nxt, 0, 1 - slot):
                    copy.start()

            wgb_ref[...] = wg_st[slot].astype(BF16)
            wlb_ref[...] = wl_st[slot].astype(BF16)
            wdb_ref[...] = wd_st[slot].astype(BF16)

            bias_g = bup_ref[0, pl.ds(j, 1), :]
            bias_l = bup_ref[0, pl.ds(nj + j, 1), :]
            for m_grains in range(1, n_grains + 1):
                @pl.when(grains == m_grains)
                def _():
                    m = m_grains * EXPERT_GRAIN
                    x = xb_ref[0:m, :]
                    for r in range(chunk):
                        row_copy(nxt, j, r).start()
                    hg = jnp.dot(x, wgb_ref[...], preferred_element_type=F32) + bias_g
                    hl = jnp.dot(x, wlb_ref[...], preferred_element_type=F32) + bias_l
                    g = jnp.minimum(hg, SWIGLU_LIMIT)
                    l = jnp.clip(hl, -SWIGLU_LIMIT, SWIGLU_LIMIT)
                    act = (g * jax.nn.sigmoid(SWIGLU_ALPHA * g) * (l + 1.0)).astype(BF16)
                    for n in range(0, d, EXPERT_DOWN_CHUNK):
                        cols = slice(n, min(n + EXPERT_DOWN_CHUNK, d))
                        o_ref[0:m, cols] += jnp.dot(act, wdb_ref[:, cols], preferred_element_type=F32)
            return carry

        lax.fori_loop(0, nj, hidden_tile, 0)

        @pl.when(b == n_blocks - 1)
        def _():
            drain_rows(b)


def _experts(h, gain, sorted_tok, blk_expert, blk_src, blk_grains, blk_off, w_up, b_up, w_down, b_down):
    d = h.shape[1]
    n_e, _, two_f = w_up.shape
    f = two_f // 2
    ft = min(EXPERT_FT, f)
    nj = f // ft
    assert nj % 2 == 0
    nb = blk_expert.shape[0]
    n_grains = EXPERT_ROWS // EXPERT_GRAIN
    by_expert = lambda b, be, bs, gr, off, tok: (be[b], 0, 0)
    return pl.pallas_call(
        functools.partial(_expert_body, nj=nj, ft=ft, n_blocks=nb),
        grid_spec=pltpu.PrefetchScalarGridSpec(
            num_scalar_prefetch=5,
            grid=(nb,),
            in_specs=[
                pl.BlockSpec(memory_space=pl.ANY),
                pl.BlockSpec((1, d), lambda b, *_: (0, 0)),
                pl.BlockSpec(memory_space=pl.ANY),
                pl.BlockSpec(memory_space=pl.ANY),
                pl.BlockSpec((1, 2 * nj, ft), by_expert),
                pl.BlockSpec((1, 1, d), by_expert),
            ],
            out_specs=pl.BlockSpec((EXPERT_ROWS, d), lambda b, be, bs, gr, off, tok: (bs[b], 0)),
            scratch_shapes=[
                pltpu.VMEM((EXPERT_ROWS, d), F32), pltpu.VMEM((EXPERT_ROWS, d), BF16),
                pltpu.VMEM((2, d, ft), F32), pltpu.VMEM((2, d, ft), F32), pltpu.VMEM((2, ft, d), F32),
                pltpu.VMEM((d, ft), BF16), pltpu.VMEM((d, ft), BF16), pltpu.VMEM((ft, d), BF16),
                pltpu.SemaphoreType.DMA((2, 3)), pltpu.SemaphoreType.DMA((nj,)),
            ],
        ),
        out_shape=jax.ShapeDtypeStruct((nb * EXPERT_ROWS, d), F32),
        compiler_params=_params("arbitrary"),
    )(blk_expert, blk_src, blk_grains, blk_off, sorted_tok,
      h, gain.reshape(1, d), w_up, w_down, b_up.reshape(n_e, 2 * nj, ft), b_down.reshape(n_e, 1, d))


def _combine_body(dest_ref, ys_hbm, gate_ref, h_ref, fg_ref, o_ref, buf_ref, sem, *, tt, n_tiles):
    i = pl.program_id(0)

    def row_copy(tile, slot, t, k):
        return pltpu.make_async_copy(ys_hbm.at[pl.ds(dest_ref[(tile * tt + t) * TOP_K + k], 1), :],
                                     buf_ref.at[slot, k, pl.ds(t, 1), :], sem.at[slot])

    for slot in range(2):
        @pl.when(jnp.logical_and(i < n_tiles, i % 2 == slot))
        def _():
            for t in range(tt):
                for k in range(TOP_K):
                    row_copy(i, slot, t, k).start()

    @pl.when(i >= 1)
    def _():
        slot = (i - 1) % 2

        def drain(t, carry):
            for k in range(TOP_K):
                row_copy(i - 1, slot, t, k).wait()
            return carry

        lax.fori_loop(0, tt, drain, 0, unroll=4)
        gate = gate_ref[...]
        y = h_ref[...]
        for k in range(TOP_K):
            y = y + gate[:, k:k + 1] * buf_ref[slot, k]
        o_ref[...] = _rms(y, fg_ref[...])


def _combine(ys, dest, gate, h, final_gain, *, tt=128):
    n, d = h.shape
    tt = min(tt, n)
    n_tiles = n // tt
    prev = lambda i, dst: (jnp.maximum(i - 1, 0), 0)
    return pl.pallas_call(
        functools.partial(_combine_body, tt=tt, n_tiles=n_tiles),
        grid_spec=pltpu.PrefetchScalarGridSpec(
            num_scalar_prefetch=1,
            grid=(n_tiles + 1,),
            in_specs=[
                pl.BlockSpec(memory_space=pl.ANY),
                pl.BlockSpec((tt, TOP_K), prev),
                pl.BlockSpec((tt, d), prev),
                pl.BlockSpec((1, d), lambda i, dst: (0, 0)),
            ],
            out_specs=pl.BlockSpec((tt, d), prev),
            scratch_shapes=[pltpu.VMEM((2, TOP_K, tt, d), F32), pltpu.SemaphoreType.DMA((2,))],
        ),
        out_shape=jax.ShapeDtypeStruct((n, d), F32),
        compiler_params=_params("arbitrary"),
    )(dest, ys, gate, h, final_gain.reshape(1, d))


def _routing_tables(top_idx, n_blocks):
    flat_e = top_idx.reshape(-1)
    nk = flat_e.shape[0]
    ids = jnp.arange(nk, dtype=I32)
    experts = jnp.arange(N_EXPERTS, dtype=I32)
    counts = jnp.sum((flat_e[:, None] == experts[None, :]).astype(I32), axis=0)
    first = jnp.cumsum(counts) - counts
    blocks_per_e = (counts + EXPERT_ROWS - 1) // EXPERT_ROWS
    blk_end = jnp.cumsum(blocks_per_e)
    blk_start = blk_end - blocks_per_e

    order = jnp.sort(flat_e * nk + ids)
    sorted_id = order % nk
    shift = blk_start * EXPERT_ROWS - first
    sorted_dest = ids + jnp.take(shift, order // nk)
    _, dest = lax.sort_key_val(sorted_id, sorted_dest)
    sorted_tok = jnp.concatenate([sorted_id // TOP_K, jnp.arange(EXPERT_ROWS, dtype=I32) % (nk // TOP_K)])

    b = jnp.arange(n_blocks, dtype=I32)
    used = blk_end[-1]
    src = jnp.minimum(b, used - 1)
    blk_expert = jnp.minimum(jnp.sum((blk_end[None, :] <= src[:, None]).astype(I32), axis=1), N_EXPERTS - 1)
    row0 = (src - blk_start[blk_expert]) * EXPERT_ROWS
    rows_here = jnp.clip(counts[blk_expert] - row0, 0, EXPERT_ROWS)
    grains = jnp.where(b < used, (rows_here + EXPERT_GRAIN - 1) // EXPERT_GRAIN, 0)
    blk_off = first[blk_expert] + row0
    i32 = lambda a: a.astype(I32)
    return i32(dest), i32(sorted_tok), i32(blk_expert), i32(src), i32(grains), i32(blk_off)


def kernel(x, mem, norm_mix, w_in, pool_lin, pool_scale, w_branch_a, w_branch_b, w_gate, b_gate, w_mix_out,
           xa_norm, mem_norm, xa_wq, xa_wkv, xa_wo, moe_norm, w_router, b_router, w_up, b_up, w_down, b_down,
           final_norm):
    batch, s, d = x.shape
    assert batch == 1 and norm_mix.shape[0] == 1
    h = x.reshape(s, d)
    l = 0
    bf = lambda w: w.astype(BF16)

    qkv_w = 3 * SB_WIDTH
    q_scale = jnp.concatenate([jnp.full((SB_WIDTH,), SB_HEAD_DIM ** -0.5, F32), jnp.ones((2 * SB_WIDTH,), F32)])
    qkv = _linear(h, bf(w_in[l][:, :qkv_w]), gain=norm_mix[l], col_scale=q_scale, out_dtype=BF16)
    p = _linear(h, bf(w_in[l][:, qkv_w:]), gain=norm_mix[l], out_dtype=F32)
    gates = _linear(h, bf(w_gate[l]), gain=norm_mix[l], bias=b_gate[l], act="sigmoid", out_dtype=BF16)
    o_a = _sb_attention(qkv, heads=SB_HEADS)
    o_b = _pool(p, bf(pool_lin[l]), pool_scale[l])
    h = _merge(o_a, o_b, gates, h, bf(w_branch_a[l]), bf(w_branch_b[l]), bf(w_mix_out[l]))

    kv = _linear(mem.reshape(mem.shape[1], d), bf(xa_wkv[l]), gain=mem_norm[l], out_dtype=BF16)
    h, top_idx, gate = _xattn_router(h, xa_norm[l], bf(xa_wq[l]), kv[:, :XA_WIDTH], kv[:, XA_WIDTH:],
                                     bf(xa_wo[l]), moe_norm[l], w_router[l], b_router[l])

    n_blocks = (s * TOP_K) // EXPERT_ROWS + N_EXPERTS
    dest, sorted_tok, blk_expert, blk_src, blk_grains, blk_off = _routing_tables(top_idx, n_blocks)
    ys = _experts(h, moe_norm[l], sorted_tok, blk_expert, blk_src, blk_grains, blk_off,
                  w_up[l], b_up[l], w_down[l], b_down[l])
    out = _combine(ys, dest, gate, h, final_norm)
    return out.reshape(batch, s, d)
```

```python
import functools

import jax
import jax.numpy as jnp
from jax import lax
from jax.experimental import pallas as pl
from jax.experimental.pallas import tpu as pltpu

F32 = jnp.float32
BF16 = jnp.bfloat16
U32 = jnp.uint32
I32 = jnp.int32

RMS_EPS = 1e-5

SB_HEADS = 8
SB_HEAD_DIM = 128
SB_WIDTH = SB_HEADS * SB_HEAD_DIM
POOL_WINDOWS = (2, 4, 8, 16)
POOL_GROUP_DIM = 256
POOL_WIDTH = len(POOL_WINDOWS) * POOL_GROUP_DIM
POOL_HALO = 16
XA_HEADS = 4
XA_HEAD_DIM = 128
XA_WIDTH = XA_HEADS * XA_HEAD_DIM
N_EXPERTS = 32
TOP_K = 4
SWIGLU_ALPHA = 1.702
SWIGLU_LIMIT = 7.0

VMEM_LIMIT_BYTES = 56 * 1024 * 1024

SB_DEAD_LOG = 110.0

EXPERT_ROWS = 1280
EXPERT_GRAIN = 128
EXPERT_FT = 256
EXPERT_DOWN_CHUNK = 512


def _params(*sem):
    return pltpu.CompilerParams(dimension_semantics=sem, vmem_limit_bytes=VMEM_LIMIT_BYTES)


def _rms(x, gain):
    return x * lax.rsqrt(jnp.mean(x * x, axis=-1, keepdims=True) + RMS_EPS) * gain


def _linear_body(*refs, rms, has_scale, has_bias, act):
    refs = list(refs)
    a_ref = refs.pop(0)
    g_ref = refs.pop(0) if rms else None
    w_ref = refs.pop(0)
    s_ref = refs.pop(0) if has_scale else None
    b_ref = refs.pop(0) if has_bias else None
    o_ref, abf_ref = refs

    @pl.when(pl.program_id(1) == 0)
    def _():
        a = a_ref[...].astype(F32)
        if rms:
            a = _rms(a, g_ref[...])
        abf_ref[...] = a.astype(BF16)

    acc = jnp.dot(abf_ref[...], w_ref[...], preferred_element_type=F32)
    if has_scale:
        acc = acc * s_ref[...]
    if has_bias:
        acc = acc + b_ref[...]
    if act == "sigmoid":
        acc = jax.nn.sigmoid(acc)
    o_ref[...] = acc.astype(o_ref.dtype)


def _linear(a, w, *, gain=None, col_scale=None, bias=None, act=None, out_dtype=F32, tm=1024, tn=1024):
    m, k = a.shape
    n = w.shape[1]
    tm, tn = min(tm, m), min(tn, n)
    assert m % tm == 0 and n % tn == 0
    args, specs = [a], [pl.BlockSpec((tm, k), lambda i, j: (i, 0))]
    if gain is not None:
        args.append(gain.reshape(1, k).astype(F32))
        specs.append(pl.BlockSpec((1, k), lambda i, j: (0, 0)))
    args.append(w)
    specs.append(pl.BlockSpec((k, tn), lambda i, j: (0, j)))
    for vec in (col_scale, bias):
        if vec is not None:
            args.append(vec.reshape(1, n).astype(F32))
            specs.append(pl.BlockSpec((1, tn), lambda i, j: (0, j)))
    body = functools.partial(_linear_body, rms=gain is not None, has_scale=col_scale is not None,
                             has_bias=bias is not None, act=act)
    return pl.pallas_call(
        body,
        grid=(m // tm, n // tn),
        in_specs=specs,
        out_specs=pl.BlockSpec((tm, tn), lambda i, j: (i, j)),
        out_shape=jax.ShapeDtypeStruct((m, n), out_dtype),
        scratch_shapes=[pltpu.VMEM((tm, k), BF16)],
        compiler_params=_params("parallel", "arbitrary"),
    )(*args)


def _sb_attn_body(q_ref, k_ref, v_ref, o_ref, acc_ref, run_ref, *, t, group):
    i = pl.program_id(1)
    dh = SB_HEAD_DIM
    row = lax.broadcasted_iota(I32, (t, t), 0)
    col = lax.broadcasted_iota(I32, (t, t), 1)
    causal = col < row
    later_keys = (row > col).astype(BF16)
    later_keys = jnp.concatenate([later_keys, later_keys], axis=0)
    heads = range(group)
    head_cols = [slice(h * dh, (h + 1) * dh) for h in heads]

    def tile(kb, diagonal):
        start = pl.multiple_of(kb * t, t)
        z = [lax.dot_general(q_ref[:, c], k_ref[pl.ds(start, t), c], (((1,), (1,)), ((), ())),
                             preferred_element_type=F32) for c in head_cols]
        softplus = [jnp.maximum(x, 0.0) + jnp.log(1.0 + jnp.exp(-jnp.abs(x))) for x in z]
        neg_log_keep = [jnp.where(causal, x, 0.0) for x in softplus] if diagonal else softplus
        later = []
        for x in neg_log_keep:
            hi = x.astype(BF16)
            lo = (x - hi.astype(F32)).astype(BF16)
            later.append(jnp.dot(jnp.concatenate([hi, lo], axis=1), later_keys, preferred_element_type=F32))
        run_min = None
        for h in heads:
            if diagonal:
                w = jnp.where(causal, jnp.exp(z[h] - softplus[h] - later[h]), 0.0)
                run = jnp.sum(neg_log_keep[h], axis=1, keepdims=True)
            else:
                run = run_ref[h]
                w = jnp.exp(z[h] - softplus[h] - later[h] - run)
                run = run + jnp.sum(neg_log_keep[h], axis=1, keepdims=True)
            pv = jnp.dot(w.astype(BF16), v_ref[pl.ds(start, t), head_cols[h]], preferred_element_type=F32)
            if diagonal:
                acc_ref[h] = pv
            else:
                acc_ref[h] += pv
            run_ref[h] = run
            head_min = jnp.min(run)
            run_min = head_min if run_min is None else jnp.minimum(run_min, head_min)
        return run_min

    def cond(carry):
        kb, run_min = carry
        return jnp.logical_and(kb >= 0, run_min < SB_DEAD_LOG)

    def step(carry):
        kb, _ = carry
        return kb - 1, tile(kb, False)

    lax.while_loop(cond, step, (i - 1, tile(i, True)))
    for h in range(group):
        o_ref[:, h * dh:(h + 1) * dh] = acc_ref[h].astype(o_ref.dtype)


def _sb_attention(qkv, *, heads, t=256, group=8):
    s = qkv.shape[0]
    dh = SB_HEAD_DIM
    t, group = min(t, s), min(group, heads)
    assert s % t == 0 and heads % group == 0
    n_groups = heads // group
    width = group * dh
    return pl.pallas_call(
        functools.partial(_sb_attn_body, t=t, group=group),
        grid=(n_groups, s // t),
        in_specs=[
            pl.BlockSpec((t, width), lambda g, i: (i, g)),
            pl.BlockSpec((s, width), lambda g, i: (0, n_groups + g), pipeline_mode=pl.Buffered(1)),
            pl.BlockSpec((s, width), lambda g, i: (0, 2 * n_groups + g), pipeline_mode=pl.Buffered(1)),
        ],
        out_specs=pl.BlockSpec((t, width), lambda g, i: (i, g)),
        out_shape=jax.ShapeDtypeStruct((s, heads * dh), BF16),
        scratch_shapes=[pltpu.VMEM((group, t, dh), F32), pltpu.VMEM((group, t, 1), F32)],
        compiler_params=_params("parallel", "arbitrary"),
    )(qkv, qkv, qkv)


def _pool_body(p_ref, halo_ref, lin_ref, scale_ref, o_ref, *, ts):
    i = pl.program_id(0)
    cur = p_ref[...]
    halo = jnp.where(i > 0, halo_ref[...], 0.0)
    ext = jnp.concatenate([halo, cur], axis=0)
    pos = i * ts + lax.broadcasted_iota(I32, (ts, 1), 0)
    c = POOL_GROUP_DIM
    for g, window in enumerate(POOL_WINDOWS):
        total = ext[:, g * c:(g + 1) * c]
        span = 1
        while span < window:
            total = total + pltpu.roll(total, span, 0)
            span *= 2
        total = total[POOL_HALO:]
        count = jnp.minimum(pos + 1, window).astype(F32)
        pooled = total / count - cur[:, g * c:(g + 1) * c]
        mixed = jnp.dot(pooled.astype(BF16), lin_ref[g], preferred_element_type=F32)
        o_ref[:, g * c:(g + 1) * c] = (mixed * scale_ref[:, g * c:(g + 1) * c]).astype(o_ref.dtype)


def _pool(p, pool_lin, pool_scale, *, ts=512):
    s, width = p.shape
    ts = min(ts, s)
    assert s % ts == 0 and ts % POOL_HALO == 0
    per = ts // POOL_HALO
    groups = len(POOL_WINDOWS)
    return pl.pallas_call(
        functools.partial(_pool_body, ts=ts),
        grid=(s // ts,),
        in_specs=[
            pl.BlockSpec((ts, width), lambda i: (i, 0)),
            pl.BlockSpec((POOL_HALO, width), lambda i: (jnp.maximum(i * per - 1, 0), 0)),
            pl.BlockSpec((groups, POOL_GROUP_DIM, POOL_GROUP_DIM), lambda i: (0, 0, 0)),
            pl.BlockSpec((1, width), lambda i: (0, 0)),
        ],
        out_specs=pl.BlockSpec((ts, width), lambda i: (i, 0)),
        out_shape=jax.ShapeDtypeStruct((s, width), BF16),
        compiler_params=_params("parallel"),
    )(p, p, pool_lin, pool_scale.reshape(1, width).astype(F32))


def _merge_body(oa_ref, ob_ref, ga_ref, gb_ref, h_ref, wa_ref, wb_ref, wm_ref, o_ref):
    ya = jnp.dot(oa_ref[...], wa_ref[...], preferred_element_type=F32)
    yb = jnp.dot(ob_ref[...], wb_ref[...], preferred_element_type=F32)
    merged = ga_ref[...].astype(F32) * ya + gb_ref[...].astype(F32) * yb
    o_ref[...] = h_ref[...] + jnp.dot(merged.astype(BF16), wm_ref[...], preferred_element_type=F32)


def _const_spec(shape):
    return pl.BlockSpec(shape, lambda i: (0,) * len(shape), pipeline_mode=pl.Buffered(1))


def _merge(o_a, o_b, gates, h, w_a, w_b, w_mix, *, tm=256):
    s, d = h.shape
    tm = min(tm, s)
    return pl.pallas_call(
        _merge_body,
        grid=(s // tm,),
        in_specs=[
            pl.BlockSpec((tm, o_a.shape[1]), lambda i: (i, 0)),
            pl.BlockSpec((tm, o_b.shape[1]), lambda i: (i, 0)),
            pl.BlockSpec((tm, d), lambda i: (i, 0)),
            pl.BlockSpec((tm, d), lambda i: (i, 1)),
            pl.BlockSpec((tm, d), lambda i: (i, 0)),
            _const_spec(w_a.shape), _const_spec(w_b.shape), _const_spec(w_mix.shape),
        ],
        out_specs=pl.BlockSpec((tm, d), lambda i: (i, 0)),
        out_shape=jax.ShapeDtypeStruct((s, d), F32),
        compiler_params=_params("parallel"),
    )(o_a, o_b, gates, gates, h, w_a, w_b, w_mix)


def _xattn_router_body(h_ref, xg_ref, wq_ref, k_ref, v_ref, wo_ref, mg_ref, wr_ref, br_ref,
                       h2_ref, idx_ref, gate_ref):
    h = h_ref[...]
    u = _rms(h, xg_ref[...]).astype(BF16)
    q = (jnp.dot(u, wq_ref[...], preferred_element_type=F32) * (XA_HEAD_DIM ** -0.5)).astype(BF16)
    heads = []
    for hd in range(XA_HEADS):
        sl = slice(hd * XA_HEAD_DIM, (hd + 1) * XA_HEAD_DIM)
        s = lax.dot_general(q[:, sl], k_ref[:, sl], (((1,), (1,)), ((), ())), preferred_element_type=F32)
        p = jnp.exp(s - jnp.max(s, axis=-1, keepdims=True))
        o = jnp.dot(p.astype(BF16), v_ref[:, sl], preferred_element_type=F32)
        heads.append(o / jnp.sum(p, axis=-1, keepdims=True))
    o = jnp.concatenate(heads, axis=1).astype(BF16)
    h2 = h + jnp.dot(o, wo_ref[...], preferred_element_type=F32)
    h2_ref[...] = h2

    t = _rms(h2, mg_ref[...])
    t_hi = t.astype(BF16)
    t_lo = (t - t_hi.astype(F32)).astype(BF16)
    n_e = br_ref.shape[1]
    by_hi = jnp.dot(t_hi, wr_ref[...], preferred_element_type=F32)
    by_lo = jnp.dot(t_lo, wr_ref[:, :n_e], preferred_element_type=F32)
    logits = by_hi[:, :n_e] + by_hi[:, n_e:] + by_lo + br_ref[...]
    lane = lax.broadcasted_iota(I32, logits.shape, 1).astype(F32)
    vals, idxs = [], []
    for _ in range(TOP_K):
        best = jnp.max(logits, axis=-1, keepdims=True)
        arg = jnp.min(jnp.where(logits == best, lane, float(n_e)), axis=-1, keepdims=True)
        vals.append(best)
        idxs.append(arg)
        logits = jnp.where(lane == arg, -jnp.inf, logits)
    e = jnp.exp(jnp.concatenate(vals, axis=1) - vals[0])
    gate_ref[...] = e / jnp.sum(e, axis=-1, keepdims=True)
    idx_ref[...] = jnp.concatenate(idxs, axis=1).astype(I32)


def _xattn_router(h, xa_gain, wq, kx, vx, wo, moe_gain, w_router, b_router, *, tm=512):
    s, d = h.shape
    tm = min(tm, s)
    n_e = w_router.shape[1]
    w_hi = w_router.astype(BF16)
    w_router = jnp.concatenate([w_hi, (w_router - w_hi.astype(F32)).astype(BF16)], axis=1)
    row = lambda width: pl.BlockSpec((tm, width), lambda i: (i, 0))
    return pl.pallas_call(
        _xattn_router_body,
        grid=(s // tm,),
        in_specs=[
            row(d), _const_spec((1, d)), _const_spec(wq.shape), _const_spec(kx.shape), _const_spec(vx.shape),
            _const_spec(wo.shape), _const_spec((1, d)), _const_spec(w_router.shape), _const_spec((1, n_e)),
        ],
        out_specs=[row(d), row(TOP_K), row(TOP_K)],
        out_shape=[
            jax.ShapeDtypeStruct((s, d), F32),
            jax.ShapeDtypeStruct((s, TOP_K), I32),
            jax.ShapeDtypeStruct((s, TOP_K), F32),
        ],
        compiler_params=_params("parallel"),
    )(h, xa_gain.reshape(1, d), wq, kx, vx, wo, moe_gain.reshape(1, d), w_router, b_router.reshape(1, n_e))


def _expert_body(blk_e_ref, blk_src_ref, grains_ref, off_ref, tok_ref,
                 h_hbm, gain_ref, wup_hbm, wdn_hbm, bup_ref, bdn_ref, o_ref,
                 xst_ref, xb_ref, wg_st, wl_st, wd_st, wgb_ref, wlb_ref, wdb_ref, wsem, xsem, *, nj, ft, n_blocks):
    del blk_src_ref
    b = pl.program_id(0)
    grains = grains_ref[b]
    nxt = jnp.minimum(b + 1, n_blocks - 1)
    has_next = jnp.logical_and(b + 1 < n_blocks, grains_ref[nxt] > 0)
    d = o_ref.shape[1]
    n_grains = EXPERT_ROWS // EXPERT_GRAIN
    chunk = EXPERT_ROWS // nj

    def weight_copies(blk, j, slot):
        e = blk_e_ref[blk]
        c = pl.multiple_of(j * ft, ft)
        return (
            pltpu.make_async_copy(wup_hbm.at[e, :, pl.ds(c, ft)], wg_st.at[slot], wsem.at[slot, 0]),
            pltpu.make_async_copy(wup_hbm.at[e, :, pl.ds(nj * ft + c, ft)], wl_st.at[slot], wsem.at[slot, 1]),
            pltpu.make_async_copy(wdn_hbm.at[e, pl.ds(c, ft), :], wd_st.at[slot], wsem.at[slot, 2]),
        )

    def row_copy(blk, c, r):
        row = c * chunk + r
        return pltpu.make_async_copy(h_hbm.at[pl.ds(tok_ref[off_ref[blk] + row], 1), :],
                                     xst_ref.at[pl.ds(row, 1), :], xsem.at[c])

    def drain_rows(blk):
        for c in range(nj):
            def drain(r, carry, c=c):
                row_copy(blk, c, r).wait()
                return carry

            lax.fori_loop(0, chunk, drain, 0, unroll=8)

    @pl.when(b == 0)
    def _():
        for copy in weight_copies(0, 0, 0):
            copy.start()

        def start(i, carry):
            row_copy(0, i // chunk, i % chunk).start()
            return carry

        lax.fori_loop(0, EXPERT_ROWS, start, 0, unroll=8)

    @pl.when(jnp.logical_and(grains == 0, grains_ref[jnp.maximum(b - 1, 0)] > 0))
    def _():
        drain_rows(b)

    @pl.when(grains > 0)
    def _():
        drain_rows(b)
        for q in range(n_grains):
            @pl.when(q < grains)
            def _():
                grain = pl.ds(q * EXPERT_GRAIN, EXPERT_GRAIN)
                xb_ref[grain, :] = _rms(xst_ref[grain, :], gain_ref[...]).astype(BF16)

        o_ref[...] = jnp.broadcast_to(bdn_ref[0], o_ref.shape)

        def hidden_tile(j, carry):
            slot = j % 2
            for copy in weight_copies(b, j, slot):
                copy.wait()

            @pl.when(j + 1 < nj)
            def _():
                for copy in weight_copies(b, j + 1, 1 - slot):
                    copy.start()

            @pl.when(jnp.logical_and(j + 1 == nj, has_next))
            def _():
                for copy in weight_copies(nxt, 0, 1 - slot):
                    copy.start()

            wgb_ref[...] = wg_st[slot].astype(BF16)
            wlb_ref[...] = wl_st[slot].astype(BF16)
            wdb_ref[...] = wd_st[slot].astype(BF16)

            bias_g = bup_ref[0, pl.ds(j, 1), :]
            bias_l = bup_ref[0, pl.ds(nj + j, 1), :]
            for m_grains in range(1, n_grains + 1):
                @pl.when(grains == m_grains)
                def _():
                    m = m_grains * EXPERT_GRAIN
                    x = xb_ref[0:m, :]
                    for r in range(chunk):
                        row_copy(nxt, j, r).start()
                    hg = jnp.dot(x, wgb_ref[...], preferred_element_type=F32) + bias_g
                    hl = jnp.dot(x, wlb_ref[...], preferred_element_type=F32) + bias_l
                    g = jnp.minimum(hg, SWIGLU_LIMIT)
                    l = jnp.clip(hl, -SWIGLU_LIMIT, SWIGLU_LIMIT)
                    act = (g * jax.nn.sigmoid(SWIGLU_ALPHA * g) * (l + 1.0)).astype(BF16)
                    for n in range(0, d, EXPERT_DOWN_CHUNK):
                        cols = slice(n, min(n + EXPERT_DOWN_CHUNK, d))
                        o_ref[0:m, cols] += jnp.dot(act, wdb_ref[:, cols], preferred_element_type=F32)
            return carry

        lax.fori_loop(0, nj, hidden_tile, 0)

        @pl.when(b == n_blocks - 1)
        def _():
            drain_rows(b)


def _experts(h, gain, sorted_tok, blk_expert, blk_src, blk_grains, blk_off, w_up, b_up, w_down, b_down):
    d = h.shape[1]
    n_e, _, two_f = w_up.shape
    f = two_f // 2
    ft = min(EXPERT_FT, f)
    nj = f // ft
    assert nj % 2 == 0
    nb = blk_expert.shape[0]
    n_grains = EXPERT_ROWS // EXPERT_GRAIN
    by_expert = lambda b, be, bs, gr, off, tok: (be[b], 0, 0)
    return pl.pallas_call(
        functools.partial(_expert_body, nj=nj, ft=ft, n_blocks=nb),
        grid_spec=pltpu.PrefetchScalarGridSpec(
            num_scalar_prefetch=5,
            grid=(nb,),
            in_specs=[
                pl.BlockSpec(memory_space=pl.ANY),
                pl.BlockSpec((1, d), lambda b, *_: (0, 0)),
                pl.BlockSpec(memory_space=pl.ANY),
                pl.BlockSpec(memory_space=pl.ANY),
                pl.BlockSpec((1, 2 * nj, ft), by_expert),
                pl.BlockSpec((1, 1, d), by_expert),
            ],
            out_specs=pl.BlockSpec((EXPERT_ROWS, d), lambda b, be, bs, gr, off, tok: (bs[b], 0)),
            scratch_shapes=[
                pltpu.VMEM((EXPERT_ROWS, d), F32), pltpu.VMEM((EXPERT_ROWS, d), BF16),
                pltpu.VMEM((2, d, ft), F32), pltpu.VMEM((2, d, ft), F32), pltpu.VMEM((2, ft, d), F32),
                pltpu.VMEM((d, ft), BF16), pltpu.VMEM((d, ft), BF16), pltpu.VMEM((ft, d), BF16),
                pltpu.SemaphoreType.DMA((2, 3)), pltpu.SemaphoreType.DMA((nj,)),
            ],
        ),
        out_shape=jax.ShapeDtypeStruct((nb * EXPERT_ROWS, d), F32),
        compiler_params=_params("arbitrary"),
    )(blk_expert, blk_src, blk_grains, blk_off, sorted_tok,
      h, gain.reshape(1, d), w_up, w_down, b_up.reshape(n_e, 2 * nj, ft), b_down.reshape(n_e, 1, d))


def _combine_body(dest_ref, ys_hbm, gate_ref, h_ref, fg_ref, o_ref, buf_ref, sem, *, tt, n_tiles):
    i = pl.program_id(0)

    def row_copy(tile, slot, t, k):
        return pltpu.make_async_copy(ys_hbm.at[pl.ds(dest_ref[(tile * tt + t) * TOP_K + k], 1), :],
                                     buf_ref.at[slot, k, pl.ds(t, 1), :], sem.at[slot])

    for slot in range(2):
        @pl.when(jnp.logical_and(i < n_tiles, i % 2 == slot))
        def _():
            for t in range(tt):
                for k in range(TOP_K):
                    row_copy(i, slot, t, k).start()

    @pl.when(i >= 1)
    def _():
        slot = (i - 1) % 2

        def drain(t, carry):
            for k in range(TOP_K):
                row_copy(i - 1, slot, t, k).wait()
            return carry

        lax.fori_loop(0, tt, drain, 0, unroll=4)
        gate = gate_ref[...]
        y = h_ref[...]
        for k in range(TOP_K):
            y = y + gate[:, k:k + 1] * buf_ref[slot, k]
        o_ref[...] = _rms(y, fg_ref[...])


def _combine(ys, dest, gate, h, final_gain, *, tt=128):
    n, d = h.shape
    tt = min(tt, n)
    n_tiles = n // tt
    prev = lambda i, dst: (jnp.maximum(i - 1, 0), 0)
    return pl.pallas_call(
        functools.partial(_combine_body, tt=tt, n_tiles=n_tiles),
        grid_spec=pltpu.PrefetchScalarGridSpec(
            num_scalar_prefetch=1,
            grid=(n_tiles + 1,),
            in_specs=[
                pl.BlockSpec(memory_space=pl.ANY),
                pl.BlockSpec((tt, TOP_K), prev),
                pl.BlockSpec((tt, d), prev),
                pl.BlockSpec((1, d), lambda i, dst: (0, 0)),
            ],
            out_specs=pl.BlockSpec((tt, d), prev),
            scratch_shapes=[pltpu.VMEM((2, TOP_K, tt, d), F32), pltpu.SemaphoreType.DMA((2,))],
        ),
        out_shape=jax.ShapeDtypeStruct((n, d), F32),
        compiler_params=_params("arbitrary"),
    )(dest, ys, gate, h, final_gain.reshape(1, d))


def _routing_tables(top_idx, n_blocks):
    flat_e = top_idx.reshape(-1)
    nk = flat_e.shape[0]
    ids = jnp.arange(nk, dtype=I32)
    experts = jnp.arange(N_EXPERTS, dtype=I32)
    counts = jnp.sum((flat_e[:, None] == experts[None, :]).astype(I32), axis=0)
    first = jnp.cumsum(counts) - counts
    blocks_per_e = (counts + EXPERT_ROWS - 1) // EXPERT_ROWS
    blk_end = jnp.cumsum(blocks_per_e)
    blk_start = blk_end - blocks_per_e

    order = jnp.sort(flat_e * nk + ids)
    sorted_id = order % nk
    shift = blk_start * EXPERT_ROWS - first
    sorted_dest = ids + jnp.take(shift, order // nk)
    _, dest = lax.sort_key_val(sorted_id, sorted_dest)
    sorted_tok = jnp.concatenate([sorted_id // TOP_K, jnp.arange(EXPERT_ROWS, dtype=I32) % (nk // TOP_K)])

    b = jnp.arange(n_blocks, dtype=I32)
    used = blk_end[-1]
    src = jnp.minimum(b, used - 1)
    blk_expert = jnp.minimum(jnp.sum((blk_end[None, :] <= src[:, None]).astype(I32), axis=1), N_EXPERTS - 1)
    row0 = (src - blk_start[blk_expert]) * EXPERT_ROWS
    rows_here = jnp.clip(counts[blk_expert] - row0, 0, EXPERT_ROWS)
    grains = jnp.where(b < used, (rows_here + EXPERT_GRAIN - 1) // EXPERT_GRAIN, 0)
    blk_off = first[blk_expert] + row0
    i32 = lambda a: a.astype(I32)
    return i32(dest), i32(sorted_tok), i32(blk_expert), i32(src), i32(grains), i32(blk_off)


def kernel(x, mem, norm_mix, w_in, pool_lin, pool_scale, w_branch_a, w_branch_b, w_gate, b_gate, w_mix_out,
           xa_norm, mem_norm, xa_wq, xa_wkv, xa_wo, moe_norm, w_router, b_router, w_up, b_up, w_down, b_down,
           final_norm):
    batch, s, d = x.shape
    assert batch == 1 and norm_mix.shape[0] == 1
    h = x.reshape(s, d)
    l = 0
    bf = lambda w: w.astype(BF16)

    qkv_w = 3 * SB_WIDTH
    q_scale = jnp.concatenate([jnp.full((SB_WIDTH,), SB_HEAD_DIM ** -0.5, F32), jnp.ones((2 * SB_WIDTH,), F32)])
    qkv = _linear(h, bf(w_in[l][:, :qkv_w]), gain=norm_mix[l], col_scale=q_scale, out_dtype=BF16)
    p = _linear(h, bf(w_in[l][:, qkv_w:]), gain=norm_mix[l], out_dtype=F32)
    gates = _linear(h, bf(w_gate[l]), gain=norm_mix[l], bias=b_gate[l], act="sigmoid", out_dtype=BF16, tn=2048)
    o_a = _sb_attention(qkv, heads=SB_HEADS)
    o_b = _pool(p, bf(pool_lin[l]), pool_scale[l])
    h = _merge(o_a, o_b, gates, h, bf(w_branch_a[l]), bf(w_branch_b[l]), bf(w_mix_out[l]))

    kv = _linear(mem.reshape(mem.shape[1], d), bf(xa_wkv[l]), gain=mem_norm[l], out_dtype=BF16)
    h, top_idx, gate = _xattn_router(h, xa_norm[l], bf(xa_wq[l]), kv[:, :XA_WIDTH], kv[:, XA_WIDTH:],
                                     bf(xa_wo[l]), moe_norm[l], w_router[l], b_router[l])

    n_blocks = (s * TOP_K) // EXPERT_ROWS + N_EXPERTS
    dest, sorted_tok, blk_expert, blk_src, blk_grains, blk_off = _routing_tables(top_idx, n_blocks)
    ys = _experts(h, moe_norm[l], sorted_tok, blk_expert, blk_src, blk_grains, blk_off,
                  w_up[l], b_up[l], w_down[l], b_down[l])
    out = _combine(ys, dest, gate, h, final_norm)
    return out.reshape(batch, s, d)
```

```python
import functools

import jax
import jax.numpy as jnp
from jax import lax
from jax.experimental import pallas as pl
from jax.experimental.pallas import tpu as pltpu

F32 = jnp.float32
BF16 = jnp.bfloat16
U32 = jnp.uint32
I32 = jnp.int32

RMS_EPS = 1e-5

SB_HEADS = 8
SB_HEAD_DIM = 128
SB_WIDTH = SB_HEADS * SB_HEAD_DIM
POOL_WINDOWS = (2, 4, 8, 16)
POOL_GROUP_DIM = 256
POOL_WIDTH = len(POOL_WINDOWS) * POOL_GROUP_DIM
POOL_HALO = 16
XA_HEADS = 4
XA_HEAD_DIM = 128
XA_WIDTH = XA_HEADS * XA_HEAD_DIM
N_EXPERTS = 32
TOP_K = 4
SWIGLU_ALPHA = 1.702
SWIGLU_LIMIT = 7.0

VMEM_LIMIT_BYTES = 56 * 1024 * 1024

SB_DEAD_LOG = 110.0

EXPERT_ROWS = 1280
EXPERT_GRAIN = 128
EXPERT_FT = 256
EXPERT_DOWN_CHUNK = 512


def _params(*sem):
    return pltpu.CompilerParams(dimension_semantics=sem, vmem_limit_bytes=VMEM_LIMIT_BYTES)


def _rms(x, gain):
    return x * lax.rsqrt(jnp.mean(x * x, axis=-1, keepdims=True) + RMS_EPS) * gain


def _linear_body(*refs, rms, has_scale, has_bias, act):
    refs = list(refs)
    a_ref = refs.pop(0)
    g_ref = refs.pop(0) if rms else None
    w_ref = refs.pop(0)
    s_ref = refs.pop(0) if has_scale else None
    b_ref = refs.pop(0) if has_bias else None
    o_ref, abf_ref = refs

    @pl.when(pl.program_id(1) == 0)
    def _():
        a = a_ref[...].astype(F32)
        if rms:
            a = _rms(a, g_ref[...])
        abf_ref[...] = a.astype(BF16)

    acc = jnp.dot(abf_ref[...], w_ref[...], preferred_element_type=F32)
    if has_scale:
        acc = acc * s_ref[...]
    if has_bias:
        acc = acc + b_ref[...]
    if act == "sigmoid":
        acc = jax.nn.sigmoid(acc)
    o_ref[...] = acc.astype(o_ref.dtype)


def _linear(a, w, *, gain=None, col_scale=None, bias=None, act=None, out_dtype=F32, tm=1024, tn=1024):
    m, k = a.shape
    n = w.shape[1]
    tm, tn = min(tm, m), min(tn, n)
    assert m % tm == 0 and n % tn == 0
    args, specs = [a], [pl.BlockSpec((tm, k), lambda i, j: (i, 0))]
    if gain is not None:
        args.append(gain.reshape(1, k).astype(F32))
        specs.append(pl.BlockSpec((1, k), lambda i, j: (0, 0)))
    args.append(w)
    specs.append(pl.BlockSpec((k, tn), lambda i, j: (0, j)))
    for vec in (col_scale, bias):
        if vec is not None:
            args.append(vec.reshape(1, n).astype(F32))
            specs.append(pl.BlockSpec((1, tn), lambda i, j: (0, j)))
    body = functools.partial(_linear_body, rms=gain is not None, has_scale=col_scale is not None,
                             has_bias=bias is not None, act=act)
    return pl.pallas_call(
        body,
        grid=(m // tm, n // tn),
        in_specs=specs,
        out_specs=pl.BlockSpec((tm, tn), lambda i, j: (i, j)),
        out_shape=jax.ShapeDtypeStruct((m, n), out_dtype),
        scratch_shapes=[pltpu.VMEM((tm, k), BF16)],
        compiler_params=_params("parallel", "arbitrary"),
    )(*args)


def _sb_attn_body(q_ref, k_ref, v_ref, o_ref, acc_ref, run_ref, *, t, group):
    i = pl.program_id(1)
    dh = SB_HEAD_DIM
    row = lax.broadcasted_iota(I32, (t, t), 0)
    col = lax.broadcasted_iota(I32, (t, t), 1)
    causal = col < row
    later_keys = (row > col).astype(BF16)
    later_keys = jnp.concatenate([later_keys, later_keys], axis=0)
    heads = range(group)
    head_cols = [slice(h * dh, (h + 1) * dh) for h in heads]

    def tile(kb, diagonal):
        start = pl.multiple_of(kb * t, t)
        z = [lax.dot_general(q_ref[:, c], k_ref[pl.ds(start, t), c], (((1,), (1,)), ((), ())),
                             preferred_element_type=F32) for c in head_cols]
        softplus = [jnp.maximum(x, 0.0) + jnp.log(1.0 + jnp.exp(-jnp.abs(x))) for x in z]
        neg_log_keep = [jnp.where(causal, x, 0.0) for x in softplus] if diagonal else softplus
        later = []
        for x in neg_log_keep:
            hi = x.astype(BF16)
            lo = (x - hi.astype(F32)).astype(BF16)
            later.append(jnp.dot(jnp.concatenate([hi, lo], axis=1), later_keys, preferred_element_type=F32))
        run_min = None
        for h in heads:
            if diagonal:
                w = jnp.where(causal, jnp.exp(z[h] - softplus[h] - later[h]), 0.0)
                run = jnp.sum(neg_log_keep[h], axis=1, keepdims=True)
            else:
                run = run_ref[h]
                w = jnp.exp(z[h] - softplus[h] - later[h] - run)
                run = run + jnp.sum(neg_log_keep[h], axis=1, keepdims=True)
            pv = jnp.dot(w.astype(BF16), v_ref[pl.ds(start, t), head_cols[h]], preferred_element_type=F32)
            if diagonal:
                acc_ref[h] = pv
            else:
                acc_ref[h] += pv
            run_ref[h] = run
            head_min = jnp.min(run)
            run_min = head_min if run_min is None else jnp.minimum(run_min, head_min)
        return run_min

    def cond(carry):
        kb, run_min = carry
        return jnp.logical_and(kb >= 0, run_min < SB_DEAD_LOG)

    def step(carry):
        kb, _ = carry
        return kb - 1, tile(kb, False)

    lax.while_loop(cond, step, (i - 1, tile(i, True)))
    for h in range(group):
        o_ref[:, h * dh:(h + 1) * dh] = acc_ref[h].astype(o_ref.dtype)


def _sb_attention(qkv, *, heads, t=256, group=8):
    s = qkv.shape[0]
    dh = SB_HEAD_DIM
    t, group = min(t, s), min(group, heads)
    assert s % t == 0 and heads % group == 0
    n_groups = heads // group
    width = group * dh
    return pl.pallas_call(
        functools.partial(_sb_attn_body, t=t, group=group),
        grid=(n_groups, s // t),
        in_specs=[
            pl.BlockSpec((t, width), lambda g, i: (i, g)),
            pl.BlockSpec((s, width), lambda g, i: (0, n_groups + g), pipeline_mode=pl.Buffered(1)),
            pl.BlockSpec((s, width), lambda g, i: (0, 2 * n_groups + g), pipeline_mode=pl.Buffered(1)),
        ],
        out_specs=pl.BlockSpec((t, width), lambda g, i: (i, g)),
        out_shape=jax.ShapeDtypeStruct((s, heads * dh), BF16),
        scratch_shapes=[pltpu.VMEM((group, t, dh), F32), pltpu.VMEM((group, t, 1), F32)],
        compiler_params=_params("parallel", "arbitrary"),
    )(qkv, qkv, qkv)


def _pool_body(p_ref, halo_ref, lin_ref, scale_ref, o_ref, *, ts):
    i = pl.program_id(0)
    cur = p_ref[...]
    halo = jnp.where(i > 0, halo_ref[...], 0.0)
    ext = jnp.concatenate([halo, cur], axis=0)
    pos = i * ts + lax.broadcasted_iota(I32, (ts, 1), 0)
    c = POOL_GROUP_DIM
    for g, window in enumerate(POOL_WINDOWS):
        total = ext[:, g * c:(g + 1) * c]
        span = 1
        while span < window:
            total = total + pltpu.roll(total, span, 0)
            span *= 2
        total = total[POOL_HALO:]
        count = jnp.minimum(pos + 1, window).astype(F32)
        pooled = total / count - cur[:, g * c:(g + 1) * c]
        mixed = jnp.dot(pooled.astype(BF16), lin_ref[g], preferred_element_type=F32)
        o_ref[:, g * c:(g + 1) * c] = (mixed * scale_ref[:, g * c:(g + 1) * c]).astype(o_ref.dtype)


def _pool(p, pool_lin, pool_scale, *, ts=512):
    s, width = p.shape
    ts = min(ts, s)
    assert s % ts == 0 and ts % POOL_HALO == 0
    per = ts // POOL_HALO
    groups = len(POOL_WINDOWS)
    return pl.pallas_call(
        functools.partial(_pool_body, ts=ts),
        grid=(s // ts,),
        in_specs=[
            pl.BlockSpec((ts, width), lambda i: (i, 0)),
            pl.BlockSpec((POOL_HALO, width), lambda i: (jnp.maximum(i * per - 1, 0), 0)),
            pl.BlockSpec((groups, POOL_GROUP_DIM, POOL_GROUP_DIM), lambda i: (0, 0, 0)),
            pl.BlockSpec((1, width), lambda i: (0, 0)),
        ],
        out_specs=pl.BlockSpec((ts, width), lambda i: (i, 0)),
        out_shape=jax.ShapeDtypeStruct((s, width), BF16),
        compiler_params=_params("parallel"),
    )(p, p, pool_lin, pool_scale.reshape(1, width).astype(F32))


def _merge_body(oa_ref, ob_ref, ga_ref, gb_ref, h_ref, wa_ref, wb_ref, wm_ref, o_ref):
    ya = jnp.dot(oa_ref[...], wa_ref[...], preferred_element_type=F32)
    yb = jnp.dot(ob_ref[...], wb_ref[...], preferred_element_type=F32)
    merged = ga_ref[...].astype(F32) * ya + gb_ref[...].astype(F32) * yb
    o_ref[...] = h_ref[...] + jnp.dot(merged.astype(BF16), wm_ref[...], preferred_element_type=F32)


def _const_spec(shape):
    return pl.BlockSpec(shape, lambda i: (0,) * len(shape), pipeline_mode=pl.Buffered(1))


def _merge(o_a, o_b, gates, h, w_a, w_b, w_mix, *, tm=256):
    s, d = h.shape
    tm = min(tm, s)
    return pl.pallas_call(
        _merge_body,
        grid=(s // tm,),
        in_specs=[
            pl.BlockSpec((tm, o_a.shape[1]), lambda i: (i, 0)),
            pl.BlockSpec((tm, o_b.shape[1]), lambda i: (i, 0)),
            pl.BlockSpec((tm, d), lambda i: (i, 0)),
            pl.BlockSpec((tm, d), lambda i: (i, 1)),
            pl.BlockSpec((tm, d), lambda i: (i, 0)),
            _const_spec(w_a.shape), _const_spec(w_b.shape), _const_spec(w_mix.shape),
        ],
        out_specs=pl.BlockSpec((tm, d), lambda i: (i, 0)),
        out_shape=jax.ShapeDtypeStruct((s, d), F32),
        compiler_params=_params("parallel"),
    )(o_a, o_b, gates, gates, h, w_a, w_b, w_mix)


def _xattn_router_body(h_ref, xg_ref, wq_ref, k_ref, v_ref, wo_ref, mg_ref, wr_ref, br_ref,
                       h2_ref, idx_ref, gate_ref):
    h = h_ref[...]
    u = _rms(h, xg_ref[...]).astype(BF16)
    q = (jnp.dot(u, wq_ref[...], preferred_element_type=F32) * (XA_HEAD_DIM ** -0.5)).astype(BF16)
    heads = []
    for hd in range(XA_HEADS):
        sl = slice(hd * XA_HEAD_DIM, (hd + 1) * XA_HEAD_DIM)
        s = lax.dot_general(q[:, sl], k_ref[:, sl], (((1,), (1,)), ((), ())), preferred_element_type=F32)
        p = jnp.exp(s - jnp.max(s, axis=-1, keepdims=True))
        o = jnp.dot(p.astype(BF16), v_ref[:, sl], preferred_element_type=F32)
        heads.append(o / jnp.sum(p, axis=-1, keepdims=True))
    o = jnp.concatenate(heads, axis=1).astype(BF16)
    h2 = h + jnp.dot(o, wo_ref[...], preferred_element_type=F32)
    h2_ref[...] = h2

    t = _rms(h2, mg_ref[...])
    t_hi = t.astype(BF16)
    t_lo = (t - t_hi.astype(F32)).astype(BF16)
    n_e = br_ref.shape[1]
    by_hi = jnp.dot(t_hi, wr_ref[...], preferred_element_type=F32)
    by_lo = jnp.dot(t_lo, wr_ref[:, :n_e], preferred_element_type=F32)
    logits = by_hi[:, :n_e] + by_hi[:, n_e:] + by_lo + br_ref[...]
    lane = lax.broadcasted_iota(I32, logits.shape, 1).astype(F32)
    vals, idxs = [], []
    for _ in range(TOP_K):
        best = jnp.max(logits, axis=-1, keepdims=True)
        arg = jnp.min(jnp.where(logits == best, lane, float(n_e)), axis=-1, keepdims=True)
        vals.append(best)
        idxs.append(arg)
        logits = jnp.where(lane == arg, -jnp.inf, logits)
    e = jnp.exp(jnp.concatenate(vals, axis=1) - vals[0])
    gate_ref[...] = e / jnp.sum(e, axis=-1, keepdims=True)
    idx_ref[...] = jnp.concatenate(idxs, axis=1).astype(I32)


def _xattn_router(h, xa_gain, wq, kx, vx, wo, moe_gain, w_router, b_router, *, tm=512):
    s, d = h.shape
    tm = min(tm, s)
    n_e = w_router.shape[1]
    w_hi = w_router.astype(BF16)
    w_router = jnp.concatenate([w_hi, (w_router - w_hi.astype(F32)).astype(BF16)], axis=1)
    row = lambda width: pl.BlockSpec((tm, width), lambda i: (i, 0))
    return pl.pallas_call(
        _xattn_router_body,
        grid=(s // tm,),
        in_specs=[
            row(d), _const_spec((1, d)), _const_spec(wq.shape), _const_spec(kx.shape), _const_spec(vx.shape),
            _const_spec(wo.shape), _const_spec((1, d)), _const_spec(w_router.shape), _const_spec((1, n_e)),
        ],
        out_specs=[row(d), row(TOP_K), row(TOP_K)],
        out_shape=[
            jax.ShapeDtypeStruct((s, d), F32),
            jax.ShapeDtypeStruct((s, TOP_K), I32),
            jax.ShapeDtypeStruct((s, TOP_K), F32),
        ],
        compiler_params=_params("parallel"),
    )(h, xa_gain.reshape(1, d), wq, kx, vx, wo, moe_gain.reshape(1, d), w_router, b_router.reshape(1, n_e))


def _expert_body(blk_e_ref, blk_src_ref, grains_ref, off_ref, tok_ref,
                 h_hbm, gain_ref, wup_hbm, wdn_hbm, bup_ref, bdn_ref, o_ref,
                 xst_ref, xb_ref, wg_st, wl_st, wd_st, wgb_ref, wlb_ref, wdb_ref, wsem, xsem, *, nj, ft, n_blocks):
    del blk_src_ref
    b = pl.program_id(0)
    grains = grains_ref[b]
    nxt = jnp.minimum(b + 1, n_blocks - 1)
    has_next = jnp.logical_and(b + 1 < n_blocks, grains_ref[nxt] > 0)
    d = o_ref.shape[1]
    n_grains = EXPERT_ROWS // EXPERT_GRAIN
    chunk = EXPERT_ROWS // nj

    def weight_copies(blk, j, slot):
        e = blk_e_ref[blk]
        c = pl.multiple_of(j * ft, ft)
        return (
            pltpu.make_async_copy(wup_hbm.at[e, :, pl.ds(c, ft)], wg_st.at[slot], wsem.at[slot, 0]),
            pltpu.make_async_copy(wup_hbm.at[e, :, pl.ds(nj * ft + c, ft)], wl_st.at[slot], wsem.at[slot, 1]),
            pltpu.make_async_copy(wdn_hbm.at[e, pl.ds(c, ft), :], wd_st.at[slot], wsem.at[slot, 2]),
        )

    def row_copy(blk, c, r):
        row = c * chunk + r
        return pltpu.make_async_copy(h_hbm.at[pl.ds(tok_ref[off_ref[blk] + row], 1), :],
                                     xst_ref.at[pl.ds(row, 1), :], xsem.at[c])

    def drain_rows(blk):
        for c in range(nj):
            def drain(r, carry, c=c):
                row_copy(blk, c, r).wait()
                return carry

            lax.fori_loop(0, chunk, drain, 0, unroll=8)

    @pl.when(b == 0)
    def _():
        for copy in weight_copies(0, 0, 0):
            copy.start()

        def start(i, carry):
            row_copy(0, i // chunk, i % chunk).start()
            return carry

        lax.fori_loop(0, EXPERT_ROWS, start, 0, unroll=8)

    @pl.when(jnp.logical_and(grains == 0, grains_ref[jnp.maximum(b - 1, 0)] > 0))
    def _():
        drain_rows(b)

    @pl.when(grains > 0)
    def _():
        drain_rows(b)
        for q in range(n_grains):
            @pl.when(q < grains)
            def _():
                grain = pl.ds(q * EXPERT_GRAIN, EXPERT_GRAIN)
                xb_ref[grain, :] = _rms(xst_ref[grain, :], gain_ref[...]).astype(BF16)

        o_ref[...] = jnp.broadcast_to(bdn_ref[0], o_ref.shape)

        def hidden_tile(j, carry):
            slot = j % 2
            for copy in weight_copies(b, j, slot):
                copy.wait()

            @pl.when(j + 1 < nj)
            def _():
                for copy in weight_copies(b, j + 1, 1 - slot):
                    copy.start()

            @pl.when(jnp.logical_and(j + 1 == nj, has_next))
            def _():
                for copy in weight_copies(nxt, 0, 1 - slot):
                    copy.start()

            wgb_ref[...] = wg_st[slot].astype(BF16)
            wlb_ref[...] = wl_st[slot].astype(BF16)
            wdb_ref[...] = wd_st[slot].astype(BF16)

            bias_g = bup_ref[0, pl.ds(j, 1), :]
            bias_l = bup_ref[0, pl.ds(nj + j, 1), :]
            for m_grains in range(1, n_grains + 1):
                @pl.when(grains == m_grains)
                def _():
                    m = m_grains * EXPERT_GRAIN
                    x = xb_ref[0:m, :]
                    for r in range(chunk):
                        row_copy(nxt, j, r).start()
                    hg = jnp.dot(x, wgb_ref[...], preferred_element_type=F32) + bias_g
                    hl = jnp.dot(x, wlb_ref[...], preferred_element_type=F32) + bias_l
                    g = jnp.minimum(hg, SWIGLU_LIMIT)
                    l = jnp.clip(hl, -SWIGLU_LIMIT, SWIGLU_LIMIT)
                    act = (g * jax.nn.sigmoid(SWIGLU_ALPHA * g) * (l + 1.0)).astype(BF16)
                    for n in range(0, d, EXPERT_DOWN_CHUNK):
                        cols = slice(n, min(n + EXPERT_DOWN_CHUNK, d))
                        o_ref[0:m, cols] += jnp.dot(act, wdb_ref[:, cols], preferred_element_type=F32)
            return carry

        lax.fori_loop(0, nj, hidden_tile, 0)

        @pl.when(b == n_blocks - 1)
        def _():
            drain_rows(b)


def _experts(h, gain, sorted_tok, blk_expert, blk_src, blk_grains, blk_off, w_up, b_up, w_down, b_down):
    d = h.shape[1]
    n_e, _, two_f = w_up.shape
    f = two_f // 2
    ft = min(EXPERT_FT, f)
    nj = f // ft
    assert nj % 2 == 0
    nb = blk_expert.shape[0]
    n_grains = EXPERT_ROWS // EXPERT_GRAIN
    by_expert = lambda b, be, bs, gr, off, tok: (be[b], 0, 0)
    return pl.pallas_call(
        functools.partial(_expert_body, nj=nj, ft=ft, n_blocks=nb),
        grid_spec=pltpu.PrefetchScalarGridSpec(
            num_scalar_prefetch=5,
            grid=(nb,),
            in_specs=[
                pl.BlockSpec(memory_space=pl.ANY),
                pl.BlockSpec((1, d), lambda b, *_: (0, 0)),
                pl.BlockSpec(memory_space=pl.ANY),
                pl.BlockSpec(memory_space=pl.ANY),
                pl.BlockSpec((1, 2 * nj, ft), by_expert),
                pl.BlockSpec((1, 1, d), by_expert),
            ],
            out_specs=pl.BlockSpec((EXPERT_ROWS, d), lambda b, be, bs, gr, off, tok: (bs[b], 0)),
            scratch_shapes=[
                pltpu.VMEM((EXPERT_ROWS, d), F32), pltpu.VMEM((EXPERT_ROWS, d), BF16),
                pltpu.VMEM((2, d, ft), F32), pltpu.VMEM((2, d, ft), F32), pltpu.VMEM((2, ft, d), F32),
                pltpu.VMEM((d, ft), BF16), pltpu.VMEM((d, ft), BF16), pltpu.VMEM((ft, d), BF16),
                pltpu.SemaphoreType.DMA((2, 3)), pltpu.SemaphoreType.DMA((nj,)),
            ],
        ),
        out_shape=jax.ShapeDtypeStruct((nb * EXPERT_ROWS, d), F32),
        compiler_params=_params("arbitrary"),
    )(blk_expert, blk_src, blk_grains, blk_off, sorted_tok,
      h, gain.reshape(1, d), w_up, w_down, b_up.reshape(n_e, 2 * nj, ft), b_down.reshape(n_e, 1, d))


def _combine_body(dest_ref, ys_hbm, gate_ref, h_ref, fg_ref, o_ref, buf_ref, sem, *, tt, n_tiles):
    i = pl.program_id(0)

    def row_copy(tile, slot, t, k):
        return pltpu.make_async_copy(ys_hbm.at[pl.ds(dest_ref[(tile * tt + t) * TOP_K + k], 1), :],
                                     buf_ref.at[slot, k, pl.ds(t, 1), :], sem.at[slot])

    for slot in range(2):
        @pl.when(jnp.logical_and(i < n_tiles, i % 2 == slot))
        def _():
            for t in range(tt):
                for k in range(TOP_K):
                    row_copy(i, slot, t, k).start()

    @pl.when(i >= 1)
    def _():
        slot = (i - 1) % 2

        def drain(t, carry):
            for k in range(TOP_K):
                row_copy(i - 1, slot, t, k).wait()
            return carry

        lax.fori_loop(0, tt, drain, 0, unroll=4)
        gate = gate_ref[...]
        y = h_ref[...]
        for k in range(TOP_K):
            y = y + gate[:, k:k + 1] * buf_ref[slot, k]
        o_ref[...] = _rms(y, fg_ref[...])


def _combine(ys, dest, gate, h, final_gain, *, tt=256):
    n, d = h.shape
    tt = min(tt, n)
    n_tiles = n // tt
    prev = lambda i, dst: (jnp.maximum(i - 1, 0), 0)
    return pl.pallas_call(
        functools.partial(_combine_body, tt=tt, n_tiles=n_tiles),
        grid_spec=pltpu.PrefetchScalarGridSpec(
            num_scalar_prefetch=1,
            grid=(n_tiles + 1,),
            in_specs=[
                pl.BlockSpec(memory_space=pl.ANY),
                pl.BlockSpec((tt, TOP_K), prev),
                pl.BlockSpec((tt, d), prev),
                pl.BlockSpec((1, d), lambda i, dst: (0, 0)),
            ],
            out_specs=pl.BlockSpec((tt, d), prev),
            scratch_shapes=[pltpu.VMEM((2, TOP_K, tt, d), F32), pltpu.SemaphoreType.DMA((2,))],
        ),
        out_shape=jax.ShapeDtypeStruct((n, d), F32),
        compiler_params=_params("arbitrary"),
    )(dest, ys, gate, h, final_gain.reshape(1, d))


def _routing_tables(top_idx, n_blocks):
    flat_e = top_idx.reshape(-1)
    nk = flat_e.shape[0]
    ids = jnp.arange(nk, dtype=I32)
    experts = jnp.arange(N_EXPERTS, dtype=I32)
    counts = jnp.sum((flat_e[:, None] == experts[None, :]).astype(I32), axis=0)
    first = jnp.cumsum(counts) - counts
    blocks_per_e = (counts + EXPERT_ROWS - 1) // EXPERT_ROWS
    blk_end = jnp.cumsum(blocks_per_e)
    blk_start = blk_end - blocks_per_e

    order = jnp.sort(flat_e * nk + ids)
    sorted_id = order % nk
    shift = blk_start * EXPERT_ROWS - first
    step = shift - jnp.concatenate([jnp.zeros((1,), I32), shift[:-1]])
    sorted_dest = ids + jnp.sum(jnp.where(ids[:, None] >= first[None, :], step[None, :], 0), axis=1)
    _, dest = lax.sort_key_val(sorted_id, sorted_dest)
    sorted_tok = jnp.concatenate([sorted_id // TOP_K, jnp.arange(EXPERT_ROWS, dtype=I32) % (nk // TOP_K)])

    b = jnp.arange(n_blocks, dtype=I32)
    used = blk_end[-1]
    src = jnp.minimum(b, used - 1)
    blk_expert = jnp.minimum(jnp.sum((blk_end[None, :] <= src[:, None]).astype(I32), axis=1), N_EXPERTS - 1)
    row0 = (src - blk_start[blk_expert]) * EXPERT_ROWS
    rows_here = jnp.clip(counts[blk_expert] - row0, 0, EXPERT_ROWS)
    grains = jnp.where(b < used, (rows_here + EXPERT_GRAIN - 1) // EXPERT_GRAIN, 0)
    blk_off = first[blk_expert] + row0
    i32 = lambda a: a.astype(I32)
    return i32(dest), i32(sorted_tok), i32(blk_expert), i32(src), i32(grains), i32(blk_off)


def kernel(x, mem, norm_mix, w_in, pool_lin, pool_scale, w_branch_a, w_branch_b, w_gate, b_gate, w_mix_out,
           xa_norm, mem_norm, xa_wq, xa_wkv, xa_wo, moe_norm, w_router, b_router, w_up, b_up, w_down, b_down,
           final_norm):
    batch, s, d = x.shape
    assert batch == 1 and norm_mix.shape[0] == 1
    h = x.reshape(s, d)
    l = 0
    bf = lambda w: w.astype(BF16)

    qkv_w = 3 * SB_WIDTH
    q_scale = jnp.concatenate([jnp.full((SB_WIDTH,), SB_HEAD_DIM ** -0.5, F32), jnp.ones((2 * SB_WIDTH,), F32)])
    qkv = _linear(h, bf(w_in[l][:, :qkv_w]), gain=norm_mix[l], col_scale=q_scale, out_dtype=BF16)
    p = _linear(h, bf(w_in[l][:, qkv_w:]), gain=norm_mix[l], out_dtype=F32)
    gates = _linear(h, bf(w_gate[l]), gain=norm_mix[l], bias=b_gate[l], act="sigmoid", out_dtype=BF16, tn=2048)
    o_a = _sb_attention(qkv, heads=SB_HEADS)
    o_b = _pool(p, bf(pool_lin[l]), pool_scale[l])
    h = _merge(o_a, o_b, gates, h, bf(w_branch_a[l]), bf(w_branch_b[l]), bf(w_mix_out[l]))

    kv = _linear(mem.reshape(mem.shape[1], d), bf(xa_wkv[l]), gain=mem_norm[l], out_dtype=BF16)
    h, top_idx, gate = _xattn_router(h, xa_norm[l], bf(xa_wq[l]), kv[:, :XA_WIDTH], kv[:, XA_WIDTH:],
                                     bf(xa_wo[l]), moe_norm[l], w_router[l], b_router[l])

    n_blocks = (s * TOP_K) // EXPERT_ROWS + N_EXPERTS
    dest, sorted_tok, blk_expert, blk_src, blk_grains, blk_off = _routing_tables(top_idx, n_blocks)
    ys = _experts(h, moe_norm[l], sorted_tok, blk_expert, blk_src, blk_grains, blk_off,
                  w_up[l], b_up[l], w_down[l], b_down[l])
    out = _combine(ys, dest, gate, h, final_norm)
    return out.reshape(batch, s, d)
```

```python
import functools

import jax
import jax.numpy as jnp
from jax import lax
from jax.experimental import pallas as pl
from jax.experimental.pallas import tpu as pltpu

F32 = jnp.float32
BF16 = jnp.bfloat16
U32 = jnp.uint32
I32 = jnp.int32

RMS_EPS = 1e-5

SB_HEADS = 8
SB_HEAD_DIM = 128
SB_WIDTH = SB_HEADS * SB_HEAD_DIM
POOL_WINDOWS = (2, 4, 8, 16)
POOL_GROUP_DIM = 256
POOL_WIDTH = len(POOL_WINDOWS) * POOL_GROUP_DIM
POOL_HALO = 16
XA_HEADS = 4
XA_HEAD_DIM = 128
XA_WIDTH = XA_HEADS * XA_HEAD_DIM
N_EXPERTS = 32
TOP_K = 4
SWIGLU_ALPHA = 1.702
SWIGLU_LIMIT = 7.0

VMEM_LIMIT_BYTES = 56 * 1024 * 1024

SB_DEAD_LOG = 110.0

EXPERT_ROWS = 1280
EXPERT_GRAIN = 128
EXPERT_FT = 256
EXPERT_DOWN_CHUNK = 512


def _params(*sem):
    return pltpu.CompilerParams(dimension_semantics=sem, vmem_limit_bytes=VMEM_LIMIT_BYTES)


def _rms(x, gain):
    return x * lax.rsqrt(jnp.mean(x * x, axis=-1, keepdims=True) + RMS_EPS) * gain


def _linear_body(*refs, rms, has_scale, has_bias, act):
    refs = list(refs)
    a_ref = refs.pop(0)
    g_ref = refs.pop(0) if rms else None
    w_ref = refs.pop(0)
    s_ref = refs.pop(0) if has_scale else None
    b_ref = refs.pop(0) if has_bias else None
    o_ref, abf_ref = refs

    @pl.when(pl.program_id(1) == 0)
    def _():
        a = a_ref[...].astype(F32)
        if rms:
            a = _rms(a, g_ref[...])
        abf_ref[...] = a.astype(BF16)

    acc = jnp.dot(abf_ref[...], w_ref[...], preferred_element_type=F32)
    if has_scale:
        acc = acc * s_ref[...]
    if has_bias:
        acc = acc + b_ref[...]
    if act == "sigmoid":
        acc = jax.nn.sigmoid(acc)
    o_ref[...] = acc.astype(o_ref.dtype)


def _linear(a, w, *, gain=None, col_scale=None, bias=None, act=None, out_dtype=F32, tm=1024, tn=1024):
    m, k = a.shape
    n = w.shape[1]
    tm, tn = min(tm, m), min(tn, n)
    assert m % tm == 0 and n % tn == 0
    args, specs = [a], [pl.BlockSpec((tm, k), lambda i, j: (i, 0))]
    if gain is not None:
        args.append(gain.reshape(1, k).astype(F32))
        specs.append(pl.BlockSpec((1, k), lambda i, j: (0, 0)))
    args.append(w)
    specs.append(pl.BlockSpec((k, tn), lambda i, j: (0, j)))
    for vec in (col_scale, bias):
        if vec is not None:
            args.append(vec.reshape(1, n).astype(F32))
            specs.append(pl.BlockSpec((1, tn), lambda i, j: (0, j)))
    body = functools.partial(_linear_body, rms=gain is not None, has_scale=col_scale is not None,
                             has_bias=bias is not None, act=act)
    return pl.pallas_call(
        body,
        grid=(m // tm, n // tn),
        in_specs=specs,
        out_specs=pl.BlockSpec((tm, tn), lambda i, j: (i, j)),
        out_shape=jax.ShapeDtypeStruct((m, n), out_dtype),
        scratch_shapes=[pltpu.VMEM((tm, k), BF16)],
        compiler_params=_params("parallel", "arbitrary"),
    )(*args)


def _sb_attn_body(q_ref, k_ref, v_ref, o_ref, acc_ref, run_ref, *, t, group):
    i = pl.program_id(1)
    dh = SB_HEAD_DIM
    row = lax.broadcasted_iota(I32, (t, t), 0)
    col = lax.broadcasted_iota(I32, (t, t), 1)
    causal = col < row
    later_keys = (row > col).astype(BF16)
    later_keys = jnp.concatenate([later_keys, later_keys], axis=0)
    heads = range(group)
    head_cols = [slice(h * dh, (h + 1) * dh) for h in heads]

    def tile(kb, diagonal):
        start = pl.multiple_of(kb * t, t)
        z = [lax.dot_general(q_ref[:, c], k_ref[pl.ds(start, t), c], (((1,), (1,)), ((), ())),
                             preferred_element_type=F32) for c in head_cols]
        softplus = [jnp.maximum(x, 0.0) + jnp.log(1.0 + jnp.exp(-jnp.abs(x))) for x in z]
        neg_log_keep = [jnp.where(causal, x, 0.0) for x in softplus] if diagonal else softplus
        later = []
        for x in neg_log_keep:
            hi = x.astype(BF16)
            lo = (x - hi.astype(F32)).astype(BF16)
            later.append(jnp.dot(jnp.concatenate([hi, lo], axis=1), later_keys, preferred_element_type=F32))
        run_min = None
        for h in heads:
            if diagonal:
                w = jnp.where(causal, jnp.exp(z[h] - softplus[h] - later[h]), 0.0)
                run = jnp.sum(neg_log_keep[h], axis=1, keepdims=True)
            else:
                run = run_ref[h]
                w = jnp.exp(z[h] - softplus[h] - later[h] - run)
                run = run + jnp.sum(neg_log_keep[h], axis=1, keepdims=True)
            pv = jnp.dot(w.astype(BF16), v_ref[pl.ds(start, t), head_cols[h]], preferred_element_type=F32)
            if diagonal:
                acc_ref[h] = pv
            else:
                acc_ref[h] += pv
            run_ref[h] = run
            head_min = jnp.min(run)
            run_min = head_min if run_min is None else jnp.minimum(run_min, head_min)
        return run_min

    def cond(carry):
        kb, run_min = carry
        return jnp.logical_and(kb >= 0, run_min < SB_DEAD_LOG)

    def step(carry):
        kb, _ = carry
        return kb - 1, tile(kb, False)

    lax.while_loop(cond, step, (i - 1, tile(i, True)))
    for h in range(group):
        o_ref[:, h * dh:(h + 1) * dh] = acc_ref[h].astype(o_ref.dtype)


def _sb_attention(qkv, *, heads, t=256, group=8):
    s = qkv.shape[0]
    dh = SB_HEAD_DIM
    t, group = min(t, s), min(group, heads)
    assert s % t == 0 and heads % group == 0
    n_groups = heads // group
    width = group * dh
    return pl.pallas_call(
        functools.partial(_sb_attn_body, t=t, group=group),
        grid=(n_groups, s // t),
        in_specs=[
            pl.BlockSpec((t, width), lambda g, i: (i, g)),
            pl.BlockSpec((s, width), lambda g, i: (0, n_groups + g), pipeline_mode=pl.Buffered(1)),
            pl.BlockSpec((s, width), lambda g, i: (0, 2 * n_groups + g), pipeline_mode=pl.Buffered(1)),
        ],
        out_specs=pl.BlockSpec((t, width), lambda g, i: (i, g)),
        out_shape=jax.ShapeDtypeStruct((s, heads * dh), BF16),
        scratch_shapes=[pltpu.VMEM((group, t, dh), F32), pltpu.VMEM((group, t, 1), F32)],
        compiler_params=_params("parallel", "arbitrary"),
    )(qkv, qkv, qkv)


def _pool_body(p_ref, halo_ref, lin_ref, scale_ref, o_ref, *, ts):
    i = pl.program_id(0)
    cur = p_ref[...]
    halo = jnp.where(i > 0, halo_ref[...], 0.0)
    ext = jnp.concatenate([halo, cur], axis=0)
    pos = i * ts + lax.broadcasted_iota(I32, (ts, 1), 0)
    c = POOL_GROUP_DIM
    for g, window in enumerate(POOL_WINDOWS):
        total = ext[:, g * c:(g + 1) * c]
        span = 1
        while span < window:
            total = total + pltpu.roll(total, span, 0)
            span *= 2
        total = total[POOL_HALO:]
        count = jnp.minimum(pos + 1, window).astype(F32)
        pooled = total / count - cur[:, g * c:(g + 1) * c]
        mixed = jnp.dot(pooled.astype(BF16), lin_ref[g], preferred_element_type=F32)
        o_ref[:, g * c:(g + 1) * c] = (mixed * scale_ref[:, g * c:(g + 1) * c]).astype(o_ref.dtype)


def _pool(p, pool_lin, pool_scale, *, ts=512):
    s, width = p.shape
    ts = min(ts, s)
    assert s % ts == 0 and ts % POOL_HALO == 0
    per = ts // POOL_HALO
    groups = len(POOL_WINDOWS)
    return pl.pallas_call(
        functools.partial(_pool_body, ts=ts),
        grid=(s // ts,),
        in_specs=[
            pl.BlockSpec((ts, width), lambda i: (i, 0)),
            pl.BlockSpec((POOL_HALO, width), lambda i: (jnp.maximum(i * per - 1, 0), 0)),
            pl.BlockSpec((groups, POOL_GROUP_DIM, POOL_GROUP_DIM), lambda i: (0, 0, 0)),
            pl.BlockSpec((1, width), lambda i: (0, 0)),
        ],
        out_specs=pl.BlockSpec((ts, width), lambda i: (i, 0)),
        out_shape=jax.ShapeDtypeStruct((s, width), BF16),
        compiler_params=_params("parallel"),
    )(p, p, pool_lin, pool_scale.reshape(1, width).astype(F32))


def _merge_body(oa_ref, ob_ref, ga_ref, gb_ref, h_ref, wa_ref, wb_ref, wm_ref, o_ref):
    ya = jnp.dot(oa_ref[...], wa_ref[...], preferred_element_type=F32)
    yb = jnp.dot(ob_ref[...], wb_ref[...], preferred_element_type=F32)
    merged = ga_ref[...].astype(F32) * ya + gb_ref[...].astype(F32) * yb
    o_ref[...] = h_ref[...] + jnp.dot(merged.astype(BF16), wm_ref[...], preferred_element_type=F32)


def _const_spec(shape):
    return pl.BlockSpec(shape, lambda i: (0,) * len(shape), pipeline_mode=pl.Buffered(1))


def _merge(o_a, o_b, gates, h, w_a, w_b, w_mix, *, tm=256):
    s, d = h.shape
    tm = min(tm, s)
    return pl.pallas_call(
        _merge_body,
        grid=(s // tm,),
        in_specs=[
            pl.BlockSpec((tm, o_a.shape[1]), lambda i: (i, 0)),
            pl.BlockSpec((tm, o_b.shape[1]), lambda i: (i, 0)),
            pl.BlockSpec((tm, d), lambda i: (i, 0)),
            pl.BlockSpec((tm, d), lambda i: (i, 1)),
            pl.BlockSpec((tm, d), lambda i: (i, 0)),
            _const_spec(w_a.shape), _const_spec(w_b.shape), _const_spec(w_mix.shape),
        ],
        out_specs=pl.BlockSpec((tm, d), lambda i: (i, 0)),
        out_shape=jax.ShapeDtypeStruct((s, d), F32),
        compiler_params=_params("parallel"),
    )(o_a, o_b, gates, gates, h, w_a, w_b, w_mix)


def _xattn_router_body(h_ref, xg_ref, wq_ref, k_ref, v_ref, wo_ref, mg_ref, wr_ref, br_ref,
                       h2_ref, idx_ref, gate_ref):
    h = h_ref[...]
    u = _rms(h, xg_ref[...]).astype(BF16)
    q = (jnp.dot(u, wq_ref[...], preferred_element_type=F32) * (XA_HEAD_DIM ** -0.5)).astype(BF16)
    heads = []
    for hd in range(XA_HEADS):
        sl = slice(hd * XA_HEAD_DIM, (hd + 1) * XA_HEAD_DIM)
        s = lax.dot_general(q[:, sl], k_ref[:, sl], (((1,), (1,)), ((), ())), preferred_element_type=F32)
        p = jnp.exp(s - jnp.max(s, axis=-1, keepdims=True))
        o = jnp.dot(p.astype(BF16), v_ref[:, sl], preferred_element_type=F32)
        heads.append(o / jnp.sum(p, axis=-1, keepdims=True))
    o = jnp.concatenate(heads, axis=1).astype(BF16)
    h2 = h + jnp.dot(o, wo_ref[...], preferred_element_type=F32)
    h2_ref[...] = h2

    t = _rms(h2, mg_ref[...])
    t_hi = t.astype(BF16)
    t_lo = (t - t_hi.astype(F32)).astype(BF16)
    n_e = br_ref.shape[1]
    by_hi = jnp.dot(t_hi, wr_ref[...], preferred_element_type=F32)
    by_lo = jnp.dot(t_lo, wr_ref[:, :n_e], preferred_element_type=F32)
    logits = by_hi[:, :n_e] + by_hi[:, n_e:] + by_lo + br_ref[...]
    lane = lax.broadcasted_iota(I32, logits.shape, 1).astype(F32)
    vals, idxs = [], []
    for _ in range(TOP_K):
        best = jnp.max(logits, axis=-1, keepdims=True)
        arg = jnp.min(jnp.where(logits == best, lane, float(n_e)), axis=-1, keepdims=True)
        vals.append(best)
        idxs.append(arg)
        logits = jnp.where(lane == arg, -jnp.inf, logits)
    e = jnp.exp(jnp.concatenate(vals, axis=1) - vals[0])
    gate_ref[...] = e / jnp.sum(e, axis=-1, keepdims=True)
    idx_ref[...] = jnp.concatenate(idxs, axis=1).astype(I32)


def _xattn_router(h, xa_gain, wq, kx, vx, wo, moe_gain, w_router, b_router, *, tm=512):
    s, d = h.shape
    tm = min(tm, s)
    n_e = w_router.shape[1]
    w_hi = w_router.astype(BF16)
    w_router = jnp.concatenate([w_hi, (w_router - w_hi.astype(F32)).astype(BF16)], axis=1)
    row = lambda width: pl.BlockSpec((tm, width), lambda i: (i, 0))
    return pl.pallas_call(
        _xattn_router_body,
        grid=(s // tm,),
        in_specs=[
            row(d), _const_spec((1, d)), _const_spec(wq.shape), _const_spec(kx.shape), _const_spec(vx.shape),
            _const_spec(wo.shape), _const_spec((1, d)), _const_spec(w_router.shape), _const_spec((1, n_e)),
        ],
        out_specs=[row(d), row(TOP_K), row(TOP_K)],
        out_shape=[
            jax.ShapeDtypeStruct((s, d), F32),
            jax.ShapeDtypeStruct((s, TOP_K), I32),
            jax.ShapeDtypeStruct((s, TOP_K), F32),
        ],
        compiler_params=_params("parallel"),
    )(h, xa_gain.reshape(1, d), wq, kx, vx, wo, moe_gain.reshape(1, d), w_router, b_router.reshape(1, n_e))


def _expert_body(blk_e_ref, blk_src_ref, grains_ref, off_ref, tok_ref,
                 h_hbm, gain_ref, wup_hbm, wdn_hbm, bup_ref, bdn_ref, o_ref,
                 xst_ref, xb_ref, wg_st, wl_st, wd_st, wgb_ref, wlb_ref, wdb_ref, wsem, xsem, *, nj, ft, n_blocks):
    del blk_src_ref
    b = pl.program_id(0)
    grains = grains_ref[b]
    nxt = jnp.minimum(b + 1, n_blocks - 1)
    has_next = jnp.logical_and(b + 1 < n_blocks, grains_ref[nxt] > 0)
    d = o_ref.shape[1]
    n_grains = EXPERT_ROWS // EXPERT_GRAIN
    chunk = EXPERT_ROWS // nj

    def weight_copies(blk, j, slot):
        e = blk_e_ref[blk]
        c = pl.multiple_of(j * ft, ft)
        return (
            pltpu.make_async_copy(wup_hbm.at[e, :, pl.ds(c, ft)], wg_st.at[slot], wsem.at[slot, 0]),
            pltpu.make_async_copy(wup_hbm.at[e, :, pl.ds(nj * ft + c, ft)], wl_st.at[slot], wsem.at[slot, 1]),
            pltpu.make_async_copy(wdn_hbm.at[e, pl.ds(c, ft), :], wd_st.at[slot], wsem.at[slot, 2]),
        )

    def row_copy(blk, c, r):
        row = c * chunk + r
        return pltpu.make_async_copy(h_hbm.at[pl.ds(tok_ref[off_ref[blk] + row], 1), :],
                                     xst_ref.at[pl.ds(row, 1), :], xsem.at[c])

    def drain_rows(blk):
        for c in range(nj):
            def drain(r, carry, c=c):
                row_copy(blk, c, r).wait()
                return carry

            lax.fori_loop(0, chunk, drain, 0, unroll=8)

    @pl.when(b == 0)
    def _():
        for copy in weight_copies(0, 0, 0):
            copy.start(priority=1)

        def start(i, carry):
            row_copy(0, i // chunk, i % chunk).start()
            return carry

        lax.fori_loop(0, EXPERT_ROWS, start, 0, unroll=8)

    @pl.when(jnp.logical_and(grains == 0, grains_ref[jnp.maximum(b - 1, 0)] > 0))
    def _():
        drain_rows(b)

    @pl.when(grains > 0)
    def _():
        drain_rows(b)
        for q in range(n_grains):
            @pl.when(q < grains)
            def _():
                grain = pl.ds(q * EXPERT_GRAIN, EXPERT_GRAIN)
                xb_ref[grain, :] = _rms(xst_ref[grain, :], gain_ref[...]).astype(BF16)

        o_ref[...] = jnp.broadcast_to(bdn_ref[0], o_ref.shape)

        def hidden_tile(j, carry):
            slot = j % 2
            for copy in weight_copies(b, j, slot):
                copy.wait()

            @pl.when(j + 1 < nj)
            def _():
                for copy in weight_copies(b, j + 1, 1 - slot):
                    copy.start(priority=1)

            @pl.when(jnp.logical_and(j + 1 == nj, has_next))
            def _():
                for copy in weight_copies(nxt, 0, 1 - slot):
                    copy.start(priority=1)

            wgb_ref[...] = wg_st[slot].astype(BF16)
            wlb_ref[...] = wl_st[slot].astype(BF16)
            wdb_ref[...] = wd_st[slot].astype(BF16)

            bias_g = bup_ref[0, pl.ds(j, 1), :]
            bias_l = bup_ref[0, pl.ds(nj + j, 1), :]
            for m_grains in range(1, n_grains + 1):
                @pl.when(grains == m_grains)
                def _():
                    m = m_grains * EXPERT_GRAIN
                    x = xb_ref[0:m, :]
                    for r in range(chunk):
                        row_copy(nxt, j, r).start()
                    hg = jnp.dot(x, wgb_ref[...], preferred_element_type=F32) + bias_g
                    hl = jnp.dot(x, wlb_ref[...], preferred_element_type=F32) + bias_l
                    g = jnp.minimum(hg, SWIGLU_LIMIT)
                    l = jnp.clip(hl, -SWIGLU_LIMIT, SWIGLU_LIMIT)
                    act = (g * jax.nn.sigmoid(SWIGLU_ALPHA * g) * (l + 1.0)).astype(BF16)
                    for n in range(0, d, EXPERT_DOWN_CHUNK):
                        cols = slice(n, min(n + EXPERT_DOWN_CHUNK, d))
                        o_ref[0:m, cols] += jnp.dot(act, wdb_ref[:, cols], preferred_element_type=F32)
            return carry

        lax.fori_loop(0, nj, hidden_tile, 0)

        @pl.when(b == n_blocks - 1)
        def _():
            drain_rows(b)


def _experts(h, gain, sorted_tok, blk_expert, blk_src, blk_grains, blk_off, w_up, b_up, w_down, b_down):
    d = h.shape[1]
    n_e, _, two_f = w_up.shape
    f = two_f // 2
    ft = min(EXPERT_FT, f)
    nj = f // ft
    assert nj % 2 == 0
    nb = blk_expert.shape[0]
    n_grains = EXPERT_ROWS // EXPERT_GRAIN
    by_expert = lambda b, be, bs, gr, off, tok: (be[b], 0, 0)
    return pl.pallas_call(
        functools.partial(_expert_body, nj=nj, ft=ft, n_blocks=nb),
        grid_spec=pltpu.PrefetchScalarGridSpec(
            num_scalar_prefetch=5,
            grid=(nb,),
            in_specs=[
                pl.BlockSpec(memory_space=pl.ANY),
                pl.BlockSpec((1, d), lambda b, *_: (0, 0)),
                pl.BlockSpec(memory_space=pl.ANY),
                pl.BlockSpec(memory_space=pl.ANY),
                pl.BlockSpec((1, 2 * nj, ft), by_expert),
                pl.BlockSpec((1, 1, d), by_expert),
            ],
            out_specs=pl.BlockSpec((EXPERT_ROWS, d), lambda b, be, bs, gr, off, tok: (bs[b], 0)),
            scratch_shapes=[
                pltpu.VMEM((EXPERT_ROWS, d), F32), pltpu.VMEM((EXPERT_ROWS, d), BF16),
                pltpu.VMEM((2, d, ft), F32), pltpu.VMEM((2, d, ft), F32), pltpu.VMEM((2, ft, d), F32),
                pltpu.VMEM((d, ft), BF16), pltpu.VMEM((d, ft), BF16), pltpu.VMEM((ft, d), BF16),
                pltpu.SemaphoreType.DMA((2, 3)), pltpu.SemaphoreType.DMA((nj,)),
            ],
        ),
        out_shape=jax.ShapeDtypeStruct((nb * EXPERT_ROWS, d), F32),
        compiler_params=_params("arbitrary"),
    )(blk_expert, blk_src, blk_grains, blk_off, sorted_tok,
      h, gain.reshape(1, d), w_up, w_down, b_up.reshape(n_e, 2 * nj, ft), b_down.reshape(n_e, 1, d))


def _combine_body(dest_ref, ys_hbm, gate_ref, h_ref, fg_ref, o_ref, buf_ref, sem, *, tt, n_tiles):
    i = pl.program_id(0)

    def row_copy(tile, slot, t, k):
        return pltpu.make_async_copy(ys_hbm.at[pl.ds(dest_ref[(tile * tt + t) * TOP_K + k], 1), :],
                                     buf_ref.at[slot, k, pl.ds(t, 1), :], sem.at[slot])

    for slot in range(2):
        @pl.when(jnp.logical_and(i < n_tiles, i % 2 == slot))
        def _():
            for t in range(tt):
                for k in range(TOP_K):
                    row_copy(i, slot, t, k).start(priority=k % 2)

    @pl.when(i >= 1)
    def _():
        slot = (i - 1) % 2

        def drain(t, carry):
            for k in range(TOP_K):
                row_copy(i - 1, slot, t, k).wait()
            return carry

        lax.fori_loop(0, tt, drain, 0, unroll=4)
        gate = gate_ref[...]
        y = h_ref[...]
        for k in range(TOP_K):
            y = y + gate[:, k:k + 1] * buf_ref[slot, k]
        o_ref[...] = _rms(y, fg_ref[...])


def _combine(ys, dest, gate, h, final_gain, *, tt=256):
    n, d = h.shape
    tt = min(tt, n)
    n_tiles = n // tt
    prev = lambda i, dst: (jnp.maximum(i - 1, 0), 0)
    return pl.pallas_call(
        functools.partial(_combine_body, tt=tt, n_tiles=n_tiles),
        grid_spec=pltpu.PrefetchScalarGridSpec(
            num_scalar_prefetch=1,
            grid=(n_tiles + 1,),
            in_specs=[
                pl.BlockSpec(memory_space=pl.ANY),
                pl.BlockSpec((tt, TOP_K), prev),
                pl.BlockSpec((tt, d), prev),
                pl.BlockSpec((1, d), lambda i, dst: (0, 0)),
            ],
            out_specs=pl.BlockSpec((tt, d), prev),
            scratch_shapes=[pltpu.VMEM((2, TOP_K, tt, d), F32), pltpu.SemaphoreType.DMA((2,))],
        ),
        out_shape=jax.ShapeDtypeStruct((n, d), F32),
        compiler_params=_params("arbitrary"),
    )(dest, ys, gate, h, final_gain.reshape(1, d))


def _routing_tables(top_idx, n_blocks):
    flat_e = top_idx.reshape(-1)
    nk = flat_e.shape[0]
    ids = jnp.arange(nk, dtype=I32)
    experts = jnp.arange(N_EXPERTS, dtype=I32)
    counts = jnp.sum((flat_e[:, None] == experts[None, :]).astype(I32), axis=0)
    first = jnp.cumsum(counts) - counts
    blocks_per_e = (counts + EXPERT_ROWS - 1) // EXPERT_ROWS
    blk_end = jnp.cumsum(blocks_per_e)
    blk_start = blk_end - blocks_per_e

    order = jnp.sort(flat_e * nk + ids)
    sorted_id = order % nk
    shift = blk_start * EXPERT_ROWS - first
    step = shift - jnp.concatenate([jnp.zeros((1,), I32), shift[:-1]])
    sorted_dest = ids + jnp.sum(jnp.where(ids[:, None] >= first[None, :], step[None, :], 0), axis=1)
    _, dest = lax.sort_key_val(sorted_id, sorted_dest)
    sorted_tok = jnp.concatenate([sorted_id // TOP_K, jnp.arange(EXPERT_ROWS, dtype=I32) % (nk // TOP_K)])

    b = jnp.arange(n_blocks, dtype=I32)
    used = blk_end[-1]
    src = jnp.minimum(b, used - 1)
    blk_expert = jnp.minimum(jnp.sum((blk_end[None, :] <= src[:, None]).astype(I32), axis=1), N_EXPERTS - 1)
    row0 = (src - blk_start[blk_expert]) * EXPERT_ROWS
    rows_here = jnp.clip(counts[blk_expert] - row0, 0, EXPERT_ROWS)
    grains = jnp.where(b < used, (rows_here + EXPERT_GRAIN - 1) // EXPERT_GRAIN, 0)
    blk_off = first[blk_expert] + row0
    i32 = lambda a: a.astype(I32)
    return i32(dest), i32(sorted_tok), i32(blk_expert), i32(src), i32(grains), i32(blk_off)


def kernel(x, mem, norm_mix, w_in, pool_lin, pool_scale, w_branch_a, w_branch_b, w_gate, b_gate, w_mix_out,
           xa_norm, mem_norm, xa_wq, xa_wkv, xa_wo, moe_norm, w_router, b_router, w_up, b_up, w_down, b_down,
           final_norm):
    batch, s, d = x.shape
    assert batch == 1 and norm_mix.shape[0] == 1
    h = x.reshape(s, d)
    l = 0
    bf = lambda w: w.astype(BF16)

    qkv_w = 3 * SB_WIDTH
    q_scale = jnp.concatenate([jnp.full((SB_WIDTH,), SB_HEAD_DIM ** -0.5, F32), jnp.ones((2 * SB_WIDTH,), F32)])
    qkv = _linear(h, bf(w_in[l][:, :qkv_w]), gain=norm_mix[l], col_scale=q_scale, out_dtype=BF16)
    p = _linear(h, bf(w_in[l][:, qkv_w:]), gain=norm_mix[l], out_dtype=F32)
    gates = _linear(h, bf(w_gate[l]), gain=norm_mix[l], bias=b_gate[l], act="sigmoid", out_dtype=BF16, tn=2048)
    o_a = _sb_attention(qkv, heads=SB_HEADS)
    o_b = _pool(p, bf(pool_lin[l]), pool_scale[l])
    h = _merge(o_a, o_b, gates, h, bf(w_branch_a[l]), bf(w_branch_b[l]), bf(w_mix_out[l]))

    kv = _linear(mem.reshape(mem.shape[1], d), bf(xa_wkv[l]), gain=mem_norm[l], out_dtype=BF16)
    h, top_idx, gate = _xattn_router(h, xa_norm[l], bf(xa_wq[l]), kv[:, :XA_WIDTH], kv[:, XA_WIDTH:],
                                     bf(xa_wo[l]), moe_norm[l], w_router[l], b_router[l])

    n_blocks = (s * TOP_K) // EXPERT_ROWS + N_EXPERTS
    dest, sorted_tok, blk_expert, blk_src, blk_grains, blk_off = _routing_tables(top_idx, n_blocks)
    ys = _experts(h, moe_norm[l], sorted_tok, blk_expert, blk_src, blk_grains, blk_off,
                  w_up[l], b_up[l], w_down[l], b_down[l])
    out = _combine(ys, dest, gate, h, final_norm)
    return out.reshape(batch, s, d)
```
